```python
import math
import jax, jax.numpy as jnp
from jax import lax
import numpy as np

D_MODEL = 2048
BATCH = 16
SEQ = 2048
DEPTH = 4

D_MIX = D_MODEL
ATT_WIDTH = D_MIX // 2
SSM_WIDTH = D_MIX - ATT_WIDTH
ATT_HEAD_DIM = 128
ATT_HEADS = ATT_WIDTH // ATT_HEAD_DIM
ATT_BLOCK = 128
SSM_HEAD_DIM = 64
SSM_HEADS = SSM_WIDTH // SSM_HEAD_DIM
SSM_GROUPS = 2
SSM_HEADS_PER_GROUP = SSM_HEADS // SSM_GROUPS
SSM_STATE = 128
CONV_WIDTH = 4
CONV_DIM = SSM_WIDTH + 2 * SSM_GROUPS * SSM_STATE
SSD_CHUNK = 128
D_FF = -(-8 * D_MODEL // (3 * 256)) * 256
IN_DIM = 3 * ATT_WIDTH + SSM_WIDTH + CONV_DIM + SSM_HEADS
EPS = 1e-6

kernel_name = 'hymba_stickbreaking_ssd_swiglu'


def rmsnorm(x, g):
    xf = x.astype(jnp.float32)
    y = xf * lax.rsqrt(jnp.mean(xf * xf, axis=-1, keepdims=True) + EPS)
    return (y * g.astype(jnp.float32)).astype(x.dtype)


def stick_breaking_attention(q, k, v):
    bsz, s, h, dh = q.shape
    nb = s // ATT_BLOCK
    scale = dh ** -0.5
    qf = q.astype(jnp.float32).transpose(0, 2, 1, 3)
    kf = k.astype(jnp.float32).transpose(0, 2, 1, 3)
    vf = v.astype(jnp.float32).transpose(0, 2, 1, 3)
    q_blocks = qf.reshape(bsz, h, nb, ATT_BLOCK, dh).transpose(2, 0, 1, 3, 4)
    key_pos = jnp.arange(s)

    def one_block(args):
        qb, blk = args
        z = jnp.einsum('bhqd,bhkd->bhqk', qb, kf) * scale
        query_pos = blk * ATT_BLOCK + jnp.arange(ATT_BLOCK)
        mask = key_pos[None, :] < query_pos[:, None]
        log_beta = jax.nn.log_sigmoid(z)
        log_remain = jnp.where(mask, log_beta - z, 0.0)
        later = lax.cumsum(log_remain, axis=3, reverse=True) - log_remain
        weights = jnp.where(mask, jnp.exp(log_beta + later), 0.0)
        return jnp.einsum('bhqk,bhkd->bhqd', weights, vf)

    out = lax.map(one_block, (q_blocks, jnp.arange(nb)))
    out = out.transpose(1, 0, 3, 2, 4).reshape(bsz, s, h, dh)
    return out.astype(q.dtype)


def causal_depthwise_conv(u, w, bias):
    s = u.shape[1]
    up = jnp.pad(u, ((0, 0), (CONV_WIDTH - 1, 0), (0, 0)))
    out = bias
    for i in range(CONV_WIDTH):
        out = out + up[:, i:i + s] * w[i]
    return out


def ssd_chunked(x, dt, a, b_in, c_in, d_skip):
    bsz, s, h, p = x.shape
    nc = s // SSD_CHUNK
    g, hg, n = SSM_GROUPS, SSM_HEADS_PER_GROUP, SSM_STATE
    xf = x.astype(jnp.float32)
    dtf = dt.astype(jnp.float32)
    xdt = (xf * dtf[..., None]).reshape(bsz, nc, SSD_CHUNK, g, hg, p)
    bc = b_in.astype(jnp.float32).reshape(bsz, nc, SSD_CHUNK, g, n)
    cc = c_in.astype(jnp.float32).reshape(bsz, nc, SSD_CHUNK, g, n)
    da = (dtf * a.astype(jnp.float32)).reshape(bsz, nc, SSD_CHUNK, g, hg)
    a_cum = jnp.cumsum(da, axis=2)
    pos = jnp.arange(SSD_CHUNK)
    causal = (pos[:, None] >= pos[None, :])[:, :, None, None]
    seg = a_cum[:, :, :, None] - a_cum[:, :, None, :]
    decay = jnp.exp(jnp.where(causal, seg, -jnp.inf))
    cb = jnp.einsum('bclgn,bcsgn->bclsg', cc, bc)
    y_diag = jnp.einsum('bclsg,bclsgi,bcsgip->bclgip', cb, decay, xdt)
    decay_to_end = jnp.exp(a_cum[:, :, -1:] - a_cum)
    states = jnp.einsum('bcsgn,bcsgi,bcsgip->bcgipn', bc, decay_to_end, xdt)
    chunk_decay = jnp.exp(a_cum[:, :, -1])

    def step(h_prev, inp):
        st, dec = inp
        return h_prev * dec[..., None, None] + st, h_prev

    h0 = jnp.zeros((bsz, g, hg, p, n), jnp.float32)
    _, h_in = lax.scan(step, h0, (states.transpose(1, 0, 2, 3, 4, 5), chunk_decay.transpose(1, 0, 2, 3)))
    h_in = h_in.transpose(1, 0, 2, 3, 4, 5)
    y_off = jnp.einsum('bclgn,bcgipn,bclgi->bclgip', cc, h_in, jnp.exp(a_cum))
    y = (y_diag + y_off).reshape(bsz, s, h, p) + xf * d_skip.astype(jnp.float32)[:, None]
    return y


def hybrid_layer(x, norm_mix, w_in, q_gain, k_gain, conv_w, conv_b, dt_bias, a_log, d_skip,
                 attn_out_gain, ssm_out_gain, w_out, norm_ffn, w_gate, w_up, w_down):
    bsz, s, _ = x.shape
    h = rmsnorm(x, norm_mix)
    proj = h @ w_in
    splits = [ATT_WIDTH, 2 * ATT_WIDTH, 3 * ATT_WIDTH, 3 * ATT_WIDTH + SSM_WIDTH,
              3 * ATT_WIDTH + SSM_WIDTH + CONV_DIM]
    q, k, v, z, xbc, dt = jnp.split(proj, splits, axis=-1)

    q = rmsnorm(q.reshape(bsz, s, ATT_HEADS, ATT_HEAD_DIM), q_gain)
    k = rmsnorm(k.reshape(bsz, s, ATT_HEADS, ATT_HEAD_DIM), k_gain)
    v = v.reshape(bsz, s, ATT_HEADS, ATT_HEAD_DIM)
    o_att = stick_breaking_attention(q, k, v).reshape(bsz, s, ATT_WIDTH)
    o_att = rmsnorm(o_att, attn_out_gain)

    xbc = jax.nn.silu(causal_depthwise_conv(xbc, conv_w, conv_b))
    xs, bm, cm = jnp.split(xbc, [SSM_WIDTH, SSM_WIDTH + SSM_GROUPS * SSM_STATE], axis=-1)
    dt = jax.nn.softplus(dt.astype(jnp.float32) + dt_bias.astype(jnp.float32))
    a = -jnp.exp(a_log.astype(jnp.float32))
    y = ssd_chunked(xs.reshape(bsz, s, SSM_HEADS, SSM_HEAD_DIM), dt, a,
                    bm.reshape(bsz, s, SSM_GROUPS, SSM_STATE),
                    cm.reshape(bsz, s, SSM_GROUPS, SSM_STATE), d_skip)
    y = y.reshape(bsz, s, SSM_WIDTH) * jax.nn.silu(z.astype(jnp.float32))
    yg = y.reshape(bsz, s, SSM_GROUPS, SSM_WIDTH // SSM_GROUPS)
    yg = yg * lax.rsqrt(jnp.mean(yg * yg, axis=-1, keepdims=True) + EPS)
    o_ssm = (yg.reshape(bsz, s, SSM_WIDTH) * ssm_out_gain.astype(jnp.float32)).astype(x.dtype)

    x = x + jnp.concatenate([o_att, o_ssm], axis=-1) @ w_out

    h = rmsnorm(x, norm_ffn)
    x = x + (jax.nn.silu(h @ w_gate) * (h @ w_up)) @ w_down
    return x


def setup_inputs(seed: int = 0) -> dict:
    key = jax.random.key(seed)
    ks = jax.random.split(key, 18)
    f32 = jnp.float32

    def normal(k, shape, scale):
        return jax.random.normal(k, shape, f32) * scale

    x = normal(ks[0], (BATCH, SEQ, D_MODEL), 1.0)
    norm_mix = 1.0 + normal(ks[1], (DEPTH, D_MODEL), 0.02)
    w_in = normal(ks[2], (DEPTH, D_MODEL, IN_DIM), D_MODEL ** -0.5)
    q_gain = 1.0 + normal(ks[3], (DEPTH, ATT_HEAD_DIM), 0.02)
    k_gain = 1.0 + normal(ks[4], (DEPTH, ATT_HEAD_DIM), 0.02)
    conv_w = normal(ks[5], (DEPTH, CONV_WIDTH, CONV_DIM), CONV_WIDTH ** -0.5)
    conv_b = normal(ks[6], (DEPTH, CONV_DIM), 0.01)
    dt0 = jnp.exp(jax.random.uniform(ks[7], (DEPTH, SSM_HEADS), f32, math.log(1e-3), math.log(1e-1)))
    dt_bias = dt0 + jnp.log(-jnp.expm1(-dt0))
    a_log = jnp.log(jax.random.uniform(ks[8], (DEPTH, SSM_HEADS), f32, 1.0, 16.0))
    d_skip = 1.0 + normal(ks[9], (DEPTH, SSM_HEADS), 0.02)
    attn_out_gain = 1.0 + normal(ks[10], (DEPTH, ATT_WIDTH), 0.02)
    ssm_out_gain = 1.0 + normal(ks[11], (DEPTH, SSM_WIDTH), 0.02)
    w_out = normal(ks[12], (DEPTH, D_MIX, D_MODEL), D_MIX ** -0.5)
    norm_ffn = 1.0 + normal(ks[13], (DEPTH, D_MODEL), 0.02)
    w_gate = normal(ks[14], (DEPTH, D_MODEL, D_FF), D_MODEL ** -0.5)
    w_up = normal(ks[15], (DEPTH, D_MODEL, D_FF), D_MODEL ** -0.5)
    w_down = normal(ks[16], (DEPTH, D_FF, D_MODEL), D_FF ** -0.5)
    return {'x': x, 'norm_mix': norm_mix, 'w_in': w_in, 'q_gain': q_gain, 'k_gain': k_gain,
            'conv_w': conv_w, 'conv_b': conv_b, 'dt_bias': dt_bias, 'a_log': a_log, 'd_skip': d_skip,
            'attn_out_gain': attn_out_gain, 'ssm_out_gain': ssm_out_gain, 'w_out': w_out,
            'norm_ffn': norm_ffn, 'w_gate': w_gate, 'w_up': w_up, 'w_down': w_down}


def reference(x, norm_mix, w_in, q_gain, k_gain, conv_w, conv_b, dt_bias, a_log, d_skip,
              attn_out_gain, ssm_out_gain, w_out, norm_ffn, w_gate, w_up, w_down):
    for i in range(DEPTH):
        x = hybrid_layer(x, norm_mix[i], w_in[i], q_gain[i], k_gain[i], conv_w[i], conv_b[i],
                         dt_bias[i], a_log[i], d_skip[i], attn_out_gain[i], ssm_out_gain[i],
                         w_out[i], norm_ffn[i], w_gate[i], w_up[i], w_down[i])
    return x
```

```python
import functools

import jax
import jax.numpy as jnp
from jax import lax
from jax.experimental import pallas as pl
from jax.experimental.pallas import tpu as pltpu

F32 = jnp.float32
BF16 = jnp.bfloat16

D_MODEL = 2048
ATT_WIDTH = 1024
SSM_WIDTH = 1024
ATT_HEAD_DIM = 128
ATT_HEADS = ATT_WIDTH // ATT_HEAD_DIM
SSM_HEAD_DIM = 64
SSM_HEADS = SSM_WIDTH // SSM_HEAD_DIM
SSM_GROUPS = 2
SSM_STATE = 128
GROUP_WIDTH = SSM_WIDTH // SSM_GROUPS
CONV_WIDTH = 4
CONV_DIM = SSM_WIDTH + 2 * SSM_GROUPS * SSM_STATE
SSD_CHUNK = 128
D_FF = 5632
EPS = 1e-6

LANES = 128
SUBLANES = 8
VMEM_LIMIT = 56 * 1024 * 1024

MAIN_DIM = 3 * ATT_WIDTH + SSM_WIDTH + CONV_DIM
Q_OFF, K_OFF, V_OFF = 0, ATT_WIDTH, 2 * ATT_WIDTH
Z_OFF = 3 * ATT_WIDTH
XBC_OFF = Z_OFF + SSM_WIDTH


def _params(*sem):
    return pltpu.CompilerParams(dimension_semantics=sem, vmem_limit_bytes=VMEM_LIMIT)


def _inproj_kernel(x_ref, g_ref, w_ref, wdt_ref, qg_ref, kg_ref, o_ref, dt_ref, h_ref, *, tn):
    j = pl.program_id(1)
    qk_tiles = 2 * ATT_WIDTH // tn

    @pl.when(j == 0)
    def _():
        x = x_ref[...]
        ms = jnp.mean(x * x, axis=-1, keepdims=True)
        h = (x * lax.rsqrt(ms + EPS) * g_ref[...]).astype(BF16)
        h_ref[...] = h
        dt_ref[...] = jnp.dot(h, wdt_ref[...], preferred_element_type=F32)

    acc = jnp.dot(h_ref[...], w_ref[...], preferred_element_type=F32)

    @pl.when(j < qk_tiles)
    def _():
        gain = jnp.where(j < qk_tiles // 2, qg_ref[...], kg_ref[...])
        for c in range(tn // ATT_HEAD_DIM):
            seg = acc[:, c * ATT_HEAD_DIM:(c + 1) * ATT_HEAD_DIM]
            ms = jnp.mean(seg * seg, axis=-1, keepdims=True)
            o_ref[:, c * ATT_HEAD_DIM:(c + 1) * ATT_HEAD_DIM] = (
                seg * lax.rsqrt(ms + EPS) * gain).astype(BF16)

    @pl.when(j >= qk_tiles)
    def _():
        o_ref[...] = acc.astype(BF16)


def _in_proj(x2, g, w_main, w_dt, qg, kg, *, tm, tn):
    t = x2.shape[0]
    grid = (t // tm, MAIN_DIM // tn)
    return pl.pallas_call(
        functools.partial(_inproj_kernel, tn=tn),
        grid=grid,
        in_specs=[
            pl.BlockSpec((tm, D_MODEL), lambda i, j: (i, 0)),
            pl.BlockSpec((1, D_MODEL), lambda i, j: (0, 0)),
            pl.BlockSpec((D_MODEL, tn), lambda i, j: (0, j)),
            pl.BlockSpec((D_MODEL, LANES), lambda i, j: (0, 0)),
            pl.BlockSpec((1, ATT_HEAD_DIM), lambda i, j: (0, 0)),
            pl.BlockSpec((1, ATT_HEAD_DIM), lambda i, j: (0, 0)),
        ],
        out_specs=[
            pl.BlockSpec((tm, tn), lambda i, j: (i, j)),
            pl.BlockSpec((tm, LANES), lambda i, j: (i, 0)),
        ],
        out_shape=[
            jax.ShapeDtypeStruct((t, MAIN_DIM), BF16),
            jax.ShapeDtypeStruct((t, LANES), F32),
        ],
        scratch_shapes=[pltpu.VMEM((tm, D_MODEL), BF16)],
        compiler_params=_params("arbitrary", "arbitrary"),
        name="in_proj",
    )(x2, g, w_main, w_dt, qg, kg)


def _attn_kernel(q_ref, k_ref, v_ref, o_ref, *, tq, tk):
    i = pl.program_id(2)
    q = q_ref[...]
    diag_blocks = tq // tk
    rows = lax.broadcasted_iota(jnp.int32, (tk, tk), 0)
    cols = lax.broadcasted_iota(jnp.int32, (tk, tk), 1)
    later_sum = (rows > cols).astype(BF16)

    def block(jblk, carry, masked):
        r, acc = carry
        ks = pl.multiple_of(jblk * tk, tk)
        k = k_ref[pl.ds(ks, tk), :]
        v = v_ref[pl.ds(ks, tk), :]
        z = lax.dot_general(q, k, (((1,), (1,)), ((), ())), preferred_element_type=F32)
        log_beta = jnp.minimum(z, 0.0) - jnp.log(1.0 + jnp.exp(-jnp.abs(z)))
        log_remain = log_beta - z
        if masked:
            qpos = i * tq + lax.broadcasted_iota(jnp.int32, (tq, tk), 0)
            kpos = ks + lax.broadcasted_iota(jnp.int32, (tq, tk), 1)
            mask = kpos < qpos
            log_remain = jnp.where(mask, log_remain, 0.0)
        within = jnp.dot(log_remain.astype(BF16), later_sum, preferred_element_type=F32)
        w = jnp.exp(log_beta + within + r)
        if masked:
            w = jnp.where(mask, w, 0.0)
        acc = acc + jnp.dot(w.astype(BF16), v, preferred_element_type=F32)
        r = r + jnp.sum(log_remain, axis=1, keepdims=True)
        return r, acc

    carry = (jnp.zeros((tq, 1), F32), jnp.zeros((tq, ATT_HEAD_DIM), F32))
    for d in range(diag_blocks):
        carry = block((i + 1) * diag_blocks - 1 - d, carry, True)
    n_full = i * diag_blocks
    carry = lax.fori_loop(0, n_full, lambda n, c: block(n_full - 1 - n, c, False), carry)
    o_ref[...] = carry[1].astype(o_ref.dtype)


def _attention(proj3, *, tq, tk):
    b, s, _ = proj3.shape
    grid = (b, ATT_HEADS, s // tq)
    qb, kb, vb = Q_OFF // ATT_HEAD_DIM, K_OFF // ATT_HEAD_DIM, V_OFF // ATT_HEAD_DIM
    return pl.pallas_call(
        functools.partial(_attn_kernel, tq=tq, tk=tk),
        grid=grid,
        in_specs=[
            pl.BlockSpec((None, tq, ATT_HEAD_DIM), lambda bi, h, i: (bi, i, qb + h)),
            pl.BlockSpec((None, s, ATT_HEAD_DIM), lambda bi, h, i: (bi, 0, kb + h)),
            pl.BlockSpec((None, s, ATT_HEAD_DIM), lambda bi, h, i: (bi, 0, vb + h)),
        ],
        out_specs=pl.BlockSpec((None, tq, ATT_HEAD_DIM), lambda bi, h, i: (bi, i, h)),
        out_shape=jax.ShapeDtypeStruct((b, s, ATT_WIDTH), BF16),
        compiler_params=_params("arbitrary", "arbitrary", "arbitrary"),
        name="attn",
    )(proj3, proj3, proj3)


def _split3(x):
    h1 = x.astype(BF16)
    r1 = x - h1.astype(F32)
    h2 = r1.astype(BF16)
    h3 = (r1 - h2.astype(F32)).astype(BF16)
    return h1, h2, h3


def _dot3(x, m, lhs=True):
    parts = _split3(x)
    if lhs:
        outs = [jnp.dot(p, m, preferred_element_type=F32) for p in parts]
    else:
        outs = [jnp.dot(m, p, preferred_element_type=F32) for p in parts]
    return outs[0] + outs[1] + outs[2]


def _silu(x):
    return x / (1.0 + jnp.exp(-x))


def _ssd_kernel(z_ref, x0_ref, x1_ref, bc_ref, dt_ref, cw_ref, cb_ref, dtb_ref, alog_ref,
                dskip_ref, gain_ref, expand_ref, o_ref, state_ref, ext_ref):
    c = pl.program_id(1)
    L = SSD_CHUNK
    half = SSM_WIDTH // 2

    @pl.when(c == 0)
    def _():
        state_ref[...] = jnp.zeros_like(state_ref)
        ext_ref[0:SUBLANES, :] = jnp.zeros((SUBLANES, CONV_DIM), F32)

    ext_ref[SUBLANES:SUBLANES + L, 0:half] = x0_ref[...].astype(F32)
    ext_ref[SUBLANES:SUBLANES + L, half:2 * half] = x1_ref[...].astype(F32)
    ext_ref[SUBLANES:SUBLANES + L, 2 * half:CONV_DIM] = bc_ref[...].astype(F32)
    conv = cb_ref[...]
    for tap in range(CONV_WIDTH):
        start = SUBLANES - (CONV_WIDTH - 1) + tap
        conv = conv + ext_ref[start:start + L, :] * cw_ref[tap:tap + 1, :]
    ext_ref[0:SUBLANES, :] = ext_ref[L:L + SUBLANES, :]
    xbc = _silu(conv)
    xs = xbc[:, 0:SSM_WIDTH]
    b_in = xbc[:, SSM_WIDTH:SSM_WIDTH + SSM_GROUPS * SSM_STATE]
    c_in = xbc[:, SSM_WIDTH + SSM_GROUPS * SSM_STATE:CONV_DIM]

    lane = lax.broadcasted_iota(jnp.int32, (1, LANES), 1)
    a = jnp.where(lane < SSM_HEADS, -jnp.exp(alog_ref[...]), 0.0)
    dt_pre = dt_ref[...] + dtb_ref[...]
    dt = jnp.maximum(dt_pre, 0.0) + jnp.log(1.0 + jnp.exp(-jnp.abs(dt_pre)))
    da = dt * a
    rows = lax.broadcasted_iota(jnp.int32, (L, L), 0)
    cols = lax.broadcasted_iota(jnp.int32, (L, L), 1)
    causal = rows >= cols
    a_cum = _dot3(da, causal.astype(BF16), lhs=False)
    a_cum_t = a_cum.T
    dt_t = dt.T
    a_last = a_cum[L - 1:L, :]
    expand = expand_ref[...]
    decay_in = jnp.dot(jnp.exp(a_cum).astype(BF16), expand, preferred_element_type=F32)
    w_end = jnp.dot((jnp.exp(a_last - a_cum) * dt).astype(BF16), expand, preferred_element_type=F32)
    chunk_decay = _dot3(jnp.broadcast_to(jnp.exp(a_last), (SUBLANES, LANES)), expand)[0:1, :]

    xw = (xs * w_end).astype(BF16)
    lane_in_pair = lax.broadcasted_iota(jnp.int32, (L, LANES), 1)
    y_parts = []
    for g in range(SSM_GROUPS):
        bg = b_in[:, g * SSM_STATE:(g + 1) * SSM_STATE]
        cg = c_in[:, g * SSM_STATE:(g + 1) * SSM_STATE].astype(BF16)
        cb = lax.dot_general(cg, bg.astype(BF16), (((1,), (1,)), ((), ())),
                             preferred_element_type=F32)
        gcols = slice(g * GROUP_WIDTH, (g + 1) * GROUP_WIDTH)
        st = state_ref[g]
        y_off = jnp.dot(cg, st.astype(BF16), preferred_element_type=F32) * decay_in[:, gcols]
        st_new = jnp.dot(bg.T.astype(BF16), xw[:, gcols], preferred_element_type=F32)
        state_ref[g] = st * chunk_decay[:, gcols] + st_new
        heads_per_group = SSM_HEADS // SSM_GROUPS
        for pair in range(heads_per_group // 2):
            h0 = g * heads_per_group + 2 * pair
            slab = xs[:, h0 * SSM_HEAD_DIM:(h0 + 2) * SSM_HEAD_DIM]
            mats = []
            for hh in (h0, h0 + 1):
                seg = a_cum[:, hh:hh + 1] - a_cum_t[hh:hh + 1, :]
                dec = jnp.exp(jnp.where(causal, seg, -jnp.inf))
                mats.append((cb * dec * dt_t[hh:hh + 1, :]).astype(BF16))
            lhs = jnp.concatenate(mats, axis=1)
            rhs = jnp.concatenate(
                [jnp.where(lane_in_pair < SSM_HEAD_DIM, slab, 0.0),
                 jnp.where(lane_in_pair >= SSM_HEAD_DIM, slab, 0.0)], axis=0).astype(BF16)
            y_diag = jnp.dot(lhs, rhs, preferred_element_type=F32)
            lo = (h0 - g * heads_per_group) * SSM_HEAD_DIM
            y_parts.append(y_diag + y_off[:, lo:lo + LANES])
    y = jnp.concatenate(y_parts, axis=1) + xs * dskip_ref[...]
    y = y * _silu(z_ref[...].astype(F32))
    outs = []
    for g in range(SSM_GROUPS):
        yg = y[:, g * GROUP_WIDTH:(g + 1) * GROUP_WIDTH]
        ms = jnp.mean(yg * yg, axis=-1, keepdims=True)
        outs.append(yg * lax.rsqrt(ms + EPS))
    o_ref[...] = (jnp.concatenate(outs, axis=1) * gain_ref[...]).astype(o_ref.dtype)


def _ssd(proj3, dt3, conv_w, conv_b, dt_bias, a_log, d_skip_x, gain, expand):
    b, s, _ = proj3.shape
    L = SSD_CHUNK
    half = SSM_WIDTH // 2
    grid = (b, s // L)
    const = lambda shape: pl.BlockSpec(shape, lambda bi, c: (0, 0))
    return pl.pallas_call(
        _ssd_kernel,
        grid=grid,
        in_specs=[
            pl.BlockSpec((None, L, SSM_WIDTH), lambda bi, c: (bi, c, Z_OFF // SSM_WIDTH)),
            pl.BlockSpec((None, L, half), lambda bi, c: (bi, c, XBC_OFF // half)),
            pl.BlockSpec((None, L, half), lambda bi, c: (bi, c, XBC_OFF // half + 1)),
            pl.BlockSpec((None, L, half), lambda bi, c: (bi, c, XBC_OFF // half + 2)),
            pl.BlockSpec((None, L, LANES), lambda bi, c: (bi, c, 0)),
            const((CONV_WIDTH, CONV_DIM)),
            const((1, CONV_DIM)),
            const((1, LANES)),
            const((1, LANES)),
            const((1, SSM_WIDTH)),
            const((1, SSM_WIDTH)),
            const((LANES, SSM_WIDTH)),
        ],
        out_specs=pl.BlockSpec((None, L, SSM_WIDTH), lambda bi, c: (bi, c, 0)),
        out_shape=jax.ShapeDtypeStruct((b, s, SSM_WIDTH), BF16),
        scratch_shapes=[
            pltpu.VMEM((SSM_GROUPS, SSM_STATE, GROUP_WIDTH), F32),
            pltpu.VMEM((SUBLANES + SSD_CHUNK, CONV_DIM), F32),
        ],
        compiler_params=_params("arbitrary", "arbitrary"),
        name="ssd",
    )(proj3, proj3, proj3, proj3, dt3, conv_w, conv_b, dt_bias, a_log, d_skip_x, gain, expand)


def _outproj_kernel(x_ref, oa_ref, os_ref, g_ref, w_ref, o_ref):
    a = oa_ref[...].astype(F32)
    ms = jnp.mean(a * a, axis=-1, keepdims=True)
    an = (a * lax.rsqrt(ms + EPS) * g_ref[...]).astype(BF16)
    y = jnp.dot(an, w_ref[0:ATT_WIDTH, :], preferred_element_type=F32)
    y = y + jnp.dot(os_ref[...], w_ref[ATT_WIDTH:ATT_WIDTH + SSM_WIDTH, :], preferred_element_type=F32)
    o_ref[...] = x_ref[...] + y


def _out_proj(x2, o_att, o_ssm, gain, w_out, *, tm):
    t = x2.shape[0]
    return pl.pallas_call(
        _outproj_kernel,
        grid=(t // tm,),
        in_specs=[
            pl.BlockSpec((tm, D_MODEL), lambda i: (i, 0)),
            pl.BlockSpec((tm, ATT_WIDTH), lambda i: (i, 0)),
            pl.BlockSpec((tm, SSM_WIDTH), lambda i: (i, 0)),
            pl.BlockSpec((1, ATT_WIDTH), lambda i: (0, 0)),
            pl.BlockSpec((ATT_WIDTH + SSM_WIDTH, D_MODEL), lambda i: (0, 0)),
        ],
        out_specs=pl.BlockSpec((tm, D_MODEL), lambda i: (i, 0)),
        out_shape=jax.ShapeDtypeStruct((t, D_MODEL), F32),
        compiler_params=_params("arbitrary"),
        name="out_proj",
    )(x2, o_att, o_ssm, gain, w_out)


def _ffn_kernel(x_ref, g_ref, wg_ref, wu_ref, wd_ref, o_ref, h_ref):
    f = pl.program_id(1)

    @pl.when(f == 0)
    def _():
        x = x_ref[...]
        ms = jnp.mean(x * x, axis=-1, keepdims=True)
        h_ref[...] = (x * lax.rsqrt(ms + EPS) * g_ref[...]).astype(BF16)
        o_ref[...] = x

    h = h_ref[...]
    gate = jnp.dot(h, wg_ref[...], preferred_element_type=F32)
    up = jnp.dot(h, wu_ref[...], preferred_element_type=F32)
    act = (_silu(gate) * up).astype(BF16)
    o_ref[...] += jnp.dot(act, wd_ref[...], preferred_element_type=F32)


def _ffn(x2, g, w_gate, w_up, w_down, *, tm, tf):
    t = x2.shape[0]
    return pl.pallas_call(
        _ffn_kernel,
        grid=(t // tm, D_FF // tf),
        in_specs=[
            pl.BlockSpec((tm, D_MODEL), lambda i, f: (i, 0)),
            pl.BlockSpec((1, D_MODEL), lambda i, f: (0, 0)),
            pl.BlockSpec((D_MODEL, tf), lambda i, f: (0, f)),
            pl.BlockSpec((D_MODEL, tf), lambda i, f: (0, f)),
            pl.BlockSpec((tf, D_MODEL), lambda i, f: (f, 0)),
        ],
        out_specs=pl.BlockSpec((tm, D_MODEL), lambda i, f: (i, 0)),
        out_shape=jax.ShapeDtypeStruct((t, D_MODEL), F32),
        scratch_shapes=[pltpu.VMEM((tm, D_MODEL), BF16)],
        compiler_params=_params("arbitrary", "arbitrary"),
        name="ffn",
    )(x2, g, w_gate, w_up, w_down)


def _tile(pref, n):
    return pref if n % pref == 0 else n


def _layer(x, norm_mix, w_in, q_gain, k_gain, conv_w, conv_b, dt_bias, a_log, d_skip,
           attn_out_gain, ssm_out_gain, w_out, norm_ffn, w_gate, w_up, w_down, expand):
    b, s, _ = x.shape
    t = b * s
    x2 = x.reshape(t, D_MODEL)
    row = lambda v: v.reshape(1, -1).astype(F32)
    pad_lanes = lambda v: jnp.pad(v.astype(F32), (0, LANES - v.shape[0])).reshape(1, LANES)

    w_main = w_in[:, :MAIN_DIM].astype(BF16)
    w_dt = jnp.pad(w_in[:, MAIN_DIM:], ((0, 0), (0, LANES - SSM_HEADS))).astype(BF16)
    proj, dt_raw = _in_proj(x2, row(norm_mix), w_main, w_dt,
                            row(q_gain) * (ATT_HEAD_DIM ** -0.5), row(k_gain),
                            tm=_tile(1024, t), tn=512)
    proj3 = proj.reshape(b, s, MAIN_DIM)

    o_att = _attention(proj3, tq=_tile(512, s), tk=256)
    o_ssm = _ssd(proj3, dt_raw.reshape(b, s, LANES), conv_w.astype(F32), row(conv_b),
                 pad_lanes(dt_bias), pad_lanes(a_log),
                 row(jnp.repeat(d_skip, SSM_HEAD_DIM)), row(ssm_out_gain), expand)

    x2 = _out_proj(x2, o_att.reshape(t, ATT_WIDTH), o_ssm.reshape(t, SSM_WIDTH),
                   row(attn_out_gain), w_out.astype(BF16), tm=_tile(512, t))
    x2 = _ffn(x2, row(norm_ffn), w_gate.astype(BF16), w_up.astype(BF16), w_down.astype(BF16),
              tm=_tile(512, t), tf=512)
    return x2.reshape(b, s, D_MODEL)


def kernel(x, norm_mix, w_in, q_gain, k_gain, conv_w, conv_b, dt_bias, a_log, d_skip, attn_out_gain, ssm_out_gain, w_out, norm_ffn, w_gate, w_up, w_down):
    head_of_lane = jnp.arange(SSM_WIDTH, dtype=jnp.int32) // SSM_HEAD_DIM
    expand = (jnp.arange(LANES, dtype=jnp.int32)[:, None] == head_of_lane[None, :]).astype(BF16)
    for i in range(norm_mix.shape[0]):
        x = _layer(x, norm_mix[i], w_in[i], q_gain[i], k_gain[i], conv_w[i], conv_b[i],
                   dt_bias[i], a_log[i], d_skip[i], attn_out_gain[i], ssm_out_gain[i],
                   w_out[i], norm_ffn[i], w_gate[i], w_up[i], w_down[i], expand)
    return x
```

```python
import functools

import jax
import jax.numpy as jnp
from jax import lax
from jax.experimental import pallas as pl
from jax.experimental.pallas import tpu as pltpu

F32 = jnp.float32
BF16 = jnp.bfloat16

D_MODEL = 2048
ATT_WIDTH = 1024
SSM_WIDTH = 1024
ATT_HEAD_DIM = 128
ATT_HEADS = ATT_WIDTH // ATT_HEAD_DIM
SSM_HEAD_DIM = 64
SSM_HEADS = SSM_WIDTH // SSM_HEAD_DIM
SSM_GROUPS = 2
SSM_STATE = 128
GROUP_WIDTH = SSM_WIDTH // SSM_GROUPS
CONV_WIDTH = 4
CONV_DIM = SSM_WIDTH + 2 * SSM_GROUPS * SSM_STATE
SSD_CHUNK = 128
D_FF = 5632
EPS = 1e-6
LOG2_E = 1.4426950408889634

LANES = 128
SUBLANES = 8
VMEM_LIMIT = 56 * 1024 * 1024

MAIN_DIM = 3 * ATT_WIDTH + SSM_WIDTH + CONV_DIM
Q_OFF, K_OFF, V_OFF = 0, ATT_WIDTH, 2 * ATT_WIDTH
Z_OFF = 3 * ATT_WIDTH
XBC_OFF = Z_OFF + SSM_WIDTH


def _params(*sem):
    return pltpu.CompilerParams(dimension_semantics=sem, vmem_limit_bytes=VMEM_LIMIT)


def _inproj_kernel(x_ref, g_ref, w_ref, wdt_ref, qg_ref, kg_ref, o_ref, dt_ref, h_ref, *, tn):
    j = pl.program_id(1)
    qk_tiles = 2 * ATT_WIDTH // tn

    @pl.when(j == 0)
    def _():
        x = x_ref[...]
        ms = jnp.mean(x * x, axis=-1, keepdims=True)
        h = (x * lax.rsqrt(ms + EPS) * g_ref[...]).astype(BF16)
        h_ref[...] = h
        dt_ref[...] = jnp.dot(h, wdt_ref[...], preferred_element_type=F32)

    acc = jnp.dot(h_ref[...], w_ref[...], preferred_element_type=F32)

    @pl.when(j < qk_tiles)
    def _():
        gain = jnp.where(j < qk_tiles // 2, qg_ref[...], kg_ref[...])
        for c in range(tn // ATT_HEAD_DIM):
            seg = acc[:, c * ATT_HEAD_DIM:(c + 1) * ATT_HEAD_DIM]
            ms = jnp.mean(seg * seg, axis=-1, keepdims=True)
            o_ref[:, c * ATT_HEAD_DIM:(c + 1) * ATT_HEAD_DIM] = (
                seg * lax.rsqrt(ms + EPS) * gain).astype(BF16)

    @pl.when(j >= qk_tiles)
    def _():
        o_ref[...] = acc.astype(BF16)


def _in_proj(x2, g, w_main, w_dt, qg, kg, *, tm, tn):
    t = x2.shape[0]
    grid = (t // tm, MAIN_DIM // tn)
    return pl.pallas_call(
        functools.partial(_inproj_kernel, tn=tn),
        grid=grid,
        in_specs=[
            pl.BlockSpec((tm, D_MODEL), lambda i, j: (i, 0)),
            pl.BlockSpec((1, D_MODEL), lambda i, j: (0, 0)),
            pl.BlockSpec((D_MODEL, tn), lambda i, j: (0, j)),
            pl.BlockSpec((D_MODEL, LANES), lambda i, j: (0, 0)),
            pl.BlockSpec((1, ATT_HEAD_DIM), lambda i, j: (0, 0)),
            pl.BlockSpec((1, ATT_HEAD_DIM), lambda i, j: (0, 0)),
        ],
        out_specs=[
            pl.BlockSpec((tm, tn), lambda i, j: (i, j)),
            pl.BlockSpec((tm, LANES), lambda i, j: (i, 0)),
        ],
        out_shape=[
            jax.ShapeDtypeStruct((t, MAIN_DIM), BF16),
            jax.ShapeDtypeStruct((t, LANES), F32),
        ],
        scratch_shapes=[pltpu.VMEM((tm, D_MODEL), BF16)],
        compiler_params=_params("arbitrary", "arbitrary"),
        name="in_proj",
    )(x2, g, w_main, w_dt, qg, kg)


ATT_TILE = 256
ATT_HEADS_PER_STEP = 8
LOG2_WEIGHT_FLOOR = -152.0


def _attn_kernel(q_ref, k_ref, v_ref, o_ref, r_ref, acc_ref):
    t = ATT_TILE
    heads = ATT_HEADS_PER_STEP
    i = pl.program_id(2)
    rows = lax.broadcasted_iota(jnp.int32, (t, t), 0)
    cols = lax.broadcasted_iota(jnp.int32, (t, t), 1)
    later_sum = (rows > cols).astype(BF16)
    causal = cols < rows
    sign_bit = jnp.uint32(0x80000000)
    lanes = [slice(hh * ATT_HEAD_DIM, (hh + 1) * ATT_HEAD_DIM) for hh in range(heads)]

    def step(jblk, first):
        ks = pl.multiple_of(jblk * t, t)
        log_betas, log_remains, ws = [None] * heads, [None] * heads, [None] * heads
        alive = None
        for s in range(heads + 2):
            if s < heads:
                z = lax.dot_general(q_ref[:, lanes[s]], k_ref[pl.ds(ks, t), lanes[s]],
                                    (((1,), (1,)), ((), ())), preferred_element_type=F32)
                neg_abs = lax.bitcast_convert_type(lax.bitcast_convert_type(z, jnp.uint32) | sign_bit, F32)
                log_beta = jnp.minimum(z, 0.0) - jnp.log2(1.0 + jnp.exp2(neg_abs))
                log_remain = log_beta - z
                if first:
                    log_remain = jnp.where(causal, log_remain, 0.0)
                log_betas[s], log_remains[s] = log_beta, log_remain
            if 0 <= s - 1 < heads:
                hh = s - 1
                within = jnp.dot(log_remains[hh].astype(BF16), later_sum, preferred_element_type=F32)
                w = jnp.exp2(log_betas[hh] + within)
                if first:
                    w = jnp.where(causal, w, 0.0)
                ws[hh] = w.astype(BF16)
            if 0 <= s - 2 < heads:
                hh = s - 2
                pv = jnp.dot(ws[hh], v_ref[pl.ds(ks, t), lanes[hh]], preferred_element_type=F32)
                block_sum = jnp.sum(log_remains[hh], axis=1, keepdims=True)
                if first:
                    acc_ref[hh] = pv
                    r_new = block_sum
                else:
                    r_old = r_ref[hh]
                    acc_ref[hh] += jnp.exp2(r_old) * pv
                    r_new = r_old + block_sum
                r_ref[hh] = r_new
                alive = r_new if alive is None else jnp.maximum(alive, r_new)
        return (jnp.max(alive) > LOG2_WEIGHT_FLOOR).astype(jnp.int32)

    go = step(i, True)

    def cond(c):
        n, go = c
        return jnp.logical_and(n < i, go > 0)

    def body(c):
        n, _ = c
        return n + 1, step(i - 1 - n, False)

    lax.while_loop(cond, body, (jnp.int32(0), go))
    for hh in range(heads):
        o_ref[:, lanes[hh]] = acc_ref[hh].astype(o_ref.dtype)


def _attention(proj3):
    b, s, _ = proj3.shape
    width = ATT_HEADS_PER_STEP * ATT_HEAD_DIM
    grid = (b, ATT_HEADS // ATT_HEADS_PER_STEP, s // ATT_TILE)
    qb, kb, vb = Q_OFF // width, K_OFF // width, V_OFF // width
    return pl.pallas_call(
        _attn_kernel,
        grid=grid,
        in_specs=[
            pl.BlockSpec((None, ATT_TILE, width), lambda bi, h, i: (bi, i, qb + h)),
            pl.BlockSpec((None, s, width), lambda bi, h, i: (bi, 0, kb + h)),
            pl.BlockSpec((None, s, width), lambda bi, h, i: (bi, 0, vb + h)),
        ],
        out_specs=pl.BlockSpec((None, ATT_TILE, width), lambda bi, h, i: (bi, i, h)),
        out_shape=jax.ShapeDtypeStruct((b, s, ATT_WIDTH), BF16),
        scratch_shapes=[
            pltpu.VMEM((ATT_HEADS_PER_STEP, ATT_TILE, 1), F32),
            pltpu.VMEM((ATT_HEADS_PER_STEP, ATT_TILE, ATT_HEAD_DIM), F32),
        ],
        compiler_params=_params("arbitrary", "arbitrary", "arbitrary"),
        name="attn",
    )(proj3, proj3, proj3)


def _split3(x):
    h1 = x.astype(BF16)
    r1 = x - h1.astype(F32)
    h2 = r1.astype(BF16)
    h3 = (r1 - h2.astype(F32)).astype(BF16)
    return h1, h2, h3


def _dot3(x, m, lhs=True):
    parts = _split3(x)
    if lhs:
        outs = [jnp.dot(p, m, preferred_element_type=F32) for p in parts]
    else:
        outs = [jnp.dot(m, p, preferred_element_type=F32) for p in parts]
    return outs[0] + outs[1] + outs[2]


def _silu(x):
    return x / (1.0 + jnp.exp(-x))


def _ssd_kernel(z_ref, x0_ref, x1_ref, bc_ref, dt_ref, cw_ref, cb_ref, dtb_ref, alog_ref,
                dskip_ref, gain_ref, expand_ref, o_ref, state_ref, ext_ref):
    c = pl.program_id(1)
    L = SSD_CHUNK
    half = SSM_WIDTH // 2

    @pl.when(c == 0)
    def _():
        state_ref[...] = jnp.zeros_like(state_ref)
        ext_ref[0:SUBLANES, :] = jnp.zeros((SUBLANES, CONV_DIM), F32)

    ext_ref[SUBLANES:SUBLANES + L, 0:half] = x0_ref[...].astype(F32)
    ext_ref[SUBLANES:SUBLANES + L, half:2 * half] = x1_ref[...].astype(F32)
    ext_ref[SUBLANES:SUBLANES + L, 2 * half:CONV_DIM] = bc_ref[...].astype(F32)
    conv = cb_ref[...]
    for tap in range(CONV_WIDTH):
        start = SUBLANES - (CONV_WIDTH - 1) + tap
        conv = conv + ext_ref[start:start + L, :] * cw_ref[tap:tap + 1, :]
    ext_ref[0:SUBLANES, :] = ext_ref[L:L + SUBLANES, :]
    xbc = _silu(conv)
    xs = xbc[:, 0:SSM_WIDTH]
    b_in = xbc[:, SSM_WIDTH:SSM_WIDTH + SSM_GROUPS * SSM_STATE]
    c_in = xbc[:, SSM_WIDTH + SSM_GROUPS * SSM_STATE:CONV_DIM]

    lane = lax.broadcasted_iota(jnp.int32, (1, LANES), 1)
    a = jnp.where(lane < SSM_HEADS, -jnp.exp(alog_ref[...]), 0.0)
    dt_pre = dt_ref[...] + dtb_ref[...]
    dt = jnp.maximum(dt_pre, 0.0) + jnp.log(1.0 + jnp.exp(-jnp.abs(dt_pre)))
    da = dt * a
    rows = lax.broadcasted_iota(jnp.int32, (L, L), 0)
    cols = lax.broadcasted_iota(jnp.int32, (L, L), 1)
    causal = rows >= cols
    a_cum = _dot3(da, causal.astype(BF16), lhs=False)
    a_cum_t = a_cum.T
    dt_t = dt.T
    a_last = a_cum[L - 1:L, :]
    expand = expand_ref[...]
    decay_in = jnp.dot(jnp.exp(a_cum).astype(BF16), expand, preferred_element_type=F32)
    w_end = jnp.dot((jnp.exp(a_last - a_cum) * dt).astype(BF16), expand, preferred_element_type=F32)
    chunk_decay = _dot3(jnp.broadcast_to(jnp.exp(a_last), (SUBLANES, LANES)), expand)[0:1, :]

    xw = (xs * w_end).astype(BF16)
    lane_in_pair = lax.broadcasted_iota(jnp.int32, (L, LANES), 1)
    y_parts = []
    for g in range(SSM_GROUPS):
        bg = b_in[:, g * SSM_STATE:(g + 1) * SSM_STATE]
        cg = c_in[:, g * SSM_STATE:(g + 1) * SSM_STATE].astype(BF16)
        cb = lax.dot_general(cg, bg.astype(BF16), (((1,), (1,)), ((), ())),
                             preferred_element_type=F32)
        gcols = slice(g * GROUP_WIDTH, (g + 1) * GROUP_WIDTH)
        st = state_ref[g]
        y_off = jnp.dot(cg, st.astype(BF16), preferred_element_type=F32) * decay_in[:, gcols]
        st_new = jnp.dot(bg.T.astype(BF16), xw[:, gcols], preferred_element_type=F32)
        state_ref[g] = st * chunk_decay[:, gcols] + st_new
        heads_per_group = SSM_HEADS // SSM_GROUPS
        for pair in range(heads_per_group // 2):
            h0 = g * heads_per_group + 2 * pair
            slab = xs[:, h0 * SSM_HEAD_DIM:(h0 + 2) * SSM_HEAD_DIM]
            mats = []
            for hh in (h0, h0 + 1):
                seg = a_cum[:, hh:hh + 1] - a_cum_t[hh:hh + 1, :]
                dec = jnp.exp(jnp.where(causal, seg, -jnp.inf))
                mats.append((cb * dec * dt_t[hh:hh + 1, :]).astype(BF16))
            lhs = jnp.concatenate(mats, axis=1)
            rhs = jnp.concatenate(
                [jnp.where(lane_in_pair < SSM_HEAD_DIM, slab, 0.0),
                 jnp.where(lane_in_pair >= SSM_HEAD_DIM, slab, 0.0)], axis=0).astype(BF16)
            y_diag = jnp.dot(lhs, rhs, preferred_element_type=F32)
            lo = (h0 - g * heads_per_group) * SSM_HEAD_DIM
            y_parts.append(y_diag + y_off[:, lo:lo + LANES])
    y = jnp.concatenate(y_parts, axis=1) + xs * dskip_ref[...]
    y = y * _silu(z_ref[...].astype(F32))
    outs = []
    for g in range(SSM_GROUPS):
        yg = y[:, g * GROUP_WIDTH:(g + 1) * GROUP_WIDTH]
        ms = jnp.mean(yg * yg, axis=-1, keepdims=True)
        outs.append(yg * lax.rsqrt(ms + EPS))
    o_ref[...] = (jnp.concatenate(outs, axis=1) * gain_ref[...]).astype(o_ref.dtype)


def _ssd(proj3, dt3, conv_w, conv_b, dt_bias, a_log, d_skip_x, gain, expand):
    b, s, _ = proj3.shape
    L = SSD_CHUNK
    half = SSM_WIDTH // 2
    grid = (b, s // L)
    const = lambda shape: pl.BlockSpec(shape, lambda bi, c: (0, 0))
    return pl.pallas_call(
        _ssd_kernel,
        grid=grid,
        in_specs=[
            pl.BlockSpec((None, L, SSM_WIDTH), lambda bi, c: (bi, c, Z_OFF // SSM_WIDTH)),
            pl.BlockSpec((None, L, half), lambda bi, c: (bi, c, XBC_OFF // half)),
            pl.BlockSpec((None, L, half), lambda bi, c: (bi, c, XBC_OFF // half + 1)),
            pl.BlockSpec((None, L, half), lambda bi, c: (bi, c, XBC_OFF // half + 2)),
            pl.BlockSpec((None, L, LANES), lambda bi, c: (bi, c, 0)),
            const((CONV_WIDTH, CONV_DIM)),
            const((1, CONV_DIM)),
            const((1, LANES)),
            const((1, LANES)),
            const((1, SSM_WIDTH)),
            const((1, SSM_WIDTH)),
            const((LANES, SSM_WIDTH)),
        ],
        out_specs=pl.BlockSpec((None, L, SSM_WIDTH), lambda bi, c: (bi, c, 0)),
        out_shape=jax.ShapeDtypeStruct((b, s, SSM_WIDTH), BF16),
        scratch_shapes=[
            pltpu.VMEM((SSM_GROUPS, SSM_STATE, GROUP_WIDTH), F32),
            pltpu.VMEM((SUBLANES + SSD_CHUNK, CONV_DIM), F32),
        ],
        compiler_params=_params("arbitrary", "arbitrary"),
        name="ssd",
    )(proj3, proj3, proj3, proj3, dt3, conv_w, conv_b, dt_bias, a_log, d_skip_x, gain, expand)


def _outproj_kernel(x_ref, oa_ref, os_ref, g_ref, w_ref, o_ref):
    a = oa_ref[...].astype(F32)
    ms = jnp.mean(a * a, axis=-1, keepdims=True)
    an = (a * lax.rsqrt(ms + EPS) * g_ref[...]).astype(BF16)
    y = jnp.dot(an, w_ref[0:ATT_WIDTH, :], preferred_element_type=F32)
    y = y + jnp.dot(os_ref[...], w_ref[ATT_WIDTH:ATT_WIDTH + SSM_WIDTH, :], preferred_element_type=F32)
    o_ref[...] = x_ref[...] + y


def _out_proj(x2, o_att, o_ssm, gain, w_out, *, tm):
    t = x2.shape[0]
    return pl.pallas_call(
        _outproj_kernel,
        grid=(t // tm,),
        in_specs=[
            pl.BlockSpec((tm, D_MODEL), lambda i: (i, 0)),
            pl.BlockSpec((tm, ATT_WIDTH), lambda i: (i, 0)),
            pl.BlockSpec((tm, SSM_WIDTH), lambda i: (i, 0)),
            pl.BlockSpec((1, ATT_WIDTH), lambda i: (0, 0)),
            pl.BlockSpec((ATT_WIDTH + SSM_WIDTH, D_MODEL), lambda i: (0, 0)),
        ],
        out_specs=pl.BlockSpec((tm, D_MODEL), lambda i: (i, 0)),
        out_shape=jax.ShapeDtypeStruct((t, D_MODEL), F32),
        compiler_params=_params("arbitrary"),
        name="out_proj",
    )(x2, o_att, o_ssm, gain, w_out)


def _ffn_kernel(x_ref, g_ref, wg_ref, wu_ref, wd_ref, o_ref, h_ref):
    f = pl.program_id(1)

    @pl.when(f == 0)
    def _():
        x = x_ref[...]
        ms = jnp.mean(x * x, axis=-1, keepdims=True)
        h_ref[...] = (x * lax.rsqrt(ms + EPS) * g_ref[...]).astype(BF16)
        o_ref[...] = x

    h = h_ref[...]
    gate = jnp.dot(h, wg_ref[...], preferred_element_type=F32)
    up = jnp.dot(h, wu_ref[...], preferred_element_type=F32)
    act = (_silu(gate) * up).astype(BF16)
    o_ref[...] += jnp.dot(act, wd_ref[...], preferred_element_type=F32)


def _ffn(x2, g, w_gate, w_up, w_down, *, tm, tf):
    t = x2.shape[0]
    return pl.pallas_call(
        _ffn_kernel,
        grid=(t // tm, D_FF // tf),
        in_specs=[
            pl.BlockSpec((tm, D_MODEL), lambda i, f: (i, 0)),
            pl.BlockSpec((1, D_MODEL), lambda i, f: (0, 0)),
            pl.BlockSpec((D_MODEL, tf), lambda i, f: (0, f)),
            pl.BlockSpec((D_MODEL, tf), lambda i, f: (0, f)),
            pl.BlockSpec((tf, D_MODEL), lambda i, f: (f, 0)),
        ],
        out_specs=pl.BlockSpec((tm, D_MODEL), lambda i, f: (i, 0)),
        out_shape=jax.ShapeDtypeStruct((t, D_MODEL), F32),
        scratch_shapes=[pltpu.VMEM((tm, D_MODEL), BF16)],
        compiler_params=_params("arbitrary", "arbitrary"),
        name="ffn",
    )(x2, g, w_gate, w_up, w_down)


def _tile(pref, n):
    return pref if n % pref == 0 else n


def _layer(x, norm_mix, w_in, q_gain, k_gain, conv_w, conv_b, dt_bias, a_log, d_skip,
           attn_out_gain, ssm_out_gain, w_out, norm_ffn, w_gate, w_up, w_down, expand):
    b, s, _ = x.shape
    t = b * s
    x2 = x.reshape(t, D_MODEL)
    row = lambda v: v.reshape(1, -1).astype(F32)
    pad_lanes = lambda v: jnp.pad(v.astype(F32), (0, LANES - v.shape[0])).reshape(1, LANES)

    w_main = w_in[:, :MAIN_DIM].astype(BF16)
    w_dt = jnp.pad(w_in[:, MAIN_DIM:], ((0, 0), (0, LANES - SSM_HEADS))).astype(BF16)
    proj, dt_raw = _in_proj(x2, row(norm_mix), w_main, w_dt,
                            row(q_gain) * (LOG2_E * ATT_HEAD_DIM ** -0.5), row(k_gain),
                            tm=_tile(1024, t), tn=512)
    proj3 = proj.reshape(b, s, MAIN_DIM)

    o_att = _attention(proj3)
    o_ssm = _ssd(proj3, dt_raw.reshape(b, s, LANES), conv_w.astype(F32), row(conv_b),
                 pad_lanes(dt_bias), pad_lanes(a_log),
                 row(jnp.repeat(d_skip, SSM_HEAD_DIM)), row(ssm_out_gain), expand)

    x2 = _out_proj(x2, o_att.reshape(t, ATT_WIDTH), o_ssm.reshape(t, SSM_WIDTH),
                   row(attn_out_gain), w_out.astype(BF16), tm=_tile(512, t))
    x2 = _ffn(x2, row(norm_ffn), w_gate.astype(BF16), w_up.astype(BF16), w_down.astype(BF16),
              tm=_tile(1024, t), tf=512)
    return x2.reshape(b, s, D_MODEL)


def kernel(x, norm_mix, w_in, q_gain, k_gain, conv_w, conv_b, dt_bias, a_log, d_skip, attn_out_gain, ssm_out_gain, w_out, norm_ffn, w_gate, w_up, w_down):
    head_of_lane = jnp.arange(SSM_WIDTH, dtype=jnp.int32) // SSM_HEAD_DIM
    expand = (jnp.arange(LANES, dtype=jnp.int32)[:, None] == head_of_lane[None, :]).astype(BF16)
    for i in range(norm_mix.shape[0]):
        x = _layer(x, norm_mix[i], w_in[i], q_gain[i], k_gain[i], conv_w[i], conv_b[i],
                   dt_bias[i], a_log[i], d_skip[i], attn_out_gain[i], ssm_out_gain[i],
                   w_out[i], norm_ffn[i], w_gate[i], w_up[i], w_down[i], expand)
    return x
```

```python
import functools

import jax
import jax.numpy as jnp
from jax import lax
from jax.experimental import pallas as pl
from jax.experimental.pallas import tpu as pltpu

F32 = jnp.float32
BF16 = jnp.bfloat16

D_MODEL = 2048
ATT_WIDTH = 1024
SSM_WIDTH = 1024
ATT_HEAD_DIM = 128
ATT_HEADS = ATT_WIDTH // ATT_HEAD_DIM
SSM_HEAD_DIM = 64
SSM_HEADS = SSM_WIDTH // SSM_HEAD_DIM
SSM_GROUPS = 2
SSM_STATE = 128
GROUP_WIDTH = SSM_WIDTH // SSM_GROUPS
CONV_WIDTH = 4
CONV_DIM = SSM_WIDTH + 2 * SSM_GROUPS * SSM_STATE
SSD_CHUNK = 128
D_FF = 5632
EPS = 1e-6
LOG2_E = 1.4426950408889634

LANES = 128
SUBLANES = 8
VMEM_LIMIT = 56 * 1024 * 1024

MAIN_DIM = 3 * ATT_WIDTH + SSM_WIDTH + CONV_DIM
Q_OFF, K_OFF, V_OFF = 0, ATT_WIDTH, 2 * ATT_WIDTH
Z_OFF = 3 * ATT_WIDTH
XBC_OFF = Z_OFF + SSM_WIDTH


def _params(*sem):
    return pltpu.CompilerParams(dimension_semantics=sem, vmem_limit_bytes=VMEM_LIMIT)


INPROJ_SUB = 512

def _inproj_kernel(x_ref, g_ref, w_ref, wdt_ref, qg_ref, kg_ref, o_ref, dt_ref, h_ref, *, tn):
    j = pl.program_id(1)
    assert tn >= K_OFF + ATT_WIDTH

    def columns(first_tile):
        h = h_ref[...]
        for lo in range(0, tn, INPROJ_SUB):
            hi = min(lo + INPROJ_SUB, tn)
            acc = jnp.dot(h, w_ref[:, lo:hi], preferred_element_type=F32)
            for c0 in range(lo, hi, ATT_HEAD_DIM):
                seg = acc[:, c0 - lo:c0 - lo + ATT_HEAD_DIM]
                if first_tile and c0 < K_OFF + ATT_WIDTH:
                    gain = qg_ref[...] if c0 < K_OFF else kg_ref[...]
                    ms = jnp.mean(seg * seg, axis=-1, keepdims=True)
                    seg = seg * lax.rsqrt(ms + EPS) * gain
                o_ref[:, c0:c0 + ATT_HEAD_DIM] = seg.astype(BF16)

    @pl.when(j == 0)
    def _():
        x = x_ref[...]
        ms = jnp.mean(x * x, axis=-1, keepdims=True)
        h = (x * lax.rsqrt(ms + EPS) * g_ref[...]).astype(BF16)
        h_ref[...] = h
        dt_ref[...] = jnp.dot(h, wdt_ref[...], preferred_element_type=F32)
        columns(True)

    @pl.when(j > 0)
    def _():
        columns(False)


def _in_proj(x2, g, w_main, w_dt, qg, kg, *, tm, tn):
    t = x2.shape[0]
    grid = (t // tm, MAIN_DIM // tn)
    return pl.pallas_call(
        functools.partial(_inproj_kernel, tn=tn),
        grid=grid,
        in_specs=[
            pl.BlockSpec((tm, D_MODEL), lambda i, j: (i, 0)),
            pl.BlockSpec((1, D_MODEL), lambda i, j: (0, 0)),
            pl.BlockSpec((D_MODEL, tn), lambda i, j: (0, j)),
            pl.BlockSpec((D_MODEL, LANES), lambda i, j: (0, 0)),
            pl.BlockSpec((1, ATT_HEAD_DIM), lambda i, j: (0, 0)),
            pl.BlockSpec((1, ATT_HEAD_DIM), lambda i, j: (0, 0)),
        ],
        out_specs=[
            pl.BlockSpec((tm, tn), lambda i, j: (i, j)),
            pl.BlockSpec((tm, LANES), lambda i, j: (i, 0)),
        ],
        out_shape=[
            jax.ShapeDtypeStruct((t, MAIN_DIM), BF16),
            jax.ShapeDtypeStruct((t, LANES), F32),
        ],
        scratch_shapes=[pltpu.VMEM((tm, D_MODEL), BF16)],
        compiler_params=_params("arbitrary", "arbitrary"),
        name="in_proj",
    )(x2, g, w_main, w_dt, qg, kg)


ATT_TILE = 256
ATT_HEADS_PER_STEP = 8
LOG2_WEIGHT_FLOOR = -152.0


def _attn_kernel(q_ref, k_ref, v_ref, o_ref, r_ref, acc_ref):
    t = ATT_TILE
    heads = ATT_HEADS_PER_STEP
    i = pl.program_id(2)
    rows = lax.broadcasted_iota(jnp.int32, (t, t), 0)
    cols = lax.broadcasted_iota(jnp.int32, (t, t), 1)
    later_sum = (rows > cols).astype(BF16)
    causal = cols < rows
    sign_bit = jnp.uint32(0x80000000)
    lanes = [slice(hh * ATT_HEAD_DIM, (hh + 1) * ATT_HEAD_DIM) for hh in range(heads)]

    def step(jblk, first):
        ks = pl.multiple_of(jblk * t, t)
        log_betas, log_remains, ws = [None] * heads, [None] * heads, [None] * heads
        alive = None
        for s in range(heads + 2):
            if s < heads:
                z = lax.dot_general(q_ref[:, lanes[s]], k_ref[pl.ds(ks, t), lanes[s]],
                                    (((1,), (1,)), ((), ())), preferred_element_type=F32)
                neg_abs = lax.bitcast_convert_type(lax.bitcast_convert_type(z, jnp.uint32) | sign_bit, F32)
                log_beta = jnp.minimum(z, 0.0) - jnp.log2(1.0 + jnp.exp2(neg_abs))
                log_remain = log_beta - z
                if first:
                    log_remain = jnp.where(causal, log_remain, 0.0)
                log_betas[s], log_remains[s] = log_beta, log_remain
            if 0 <= s - 1 < heads:
                hh = s - 1
                within = jnp.dot(log_remains[hh].astype(BF16), later_sum, preferred_element_type=F32)
                w = jnp.exp2(log_betas[hh] + within)
                if first:
                    w = jnp.where(causal, w, 0.0)
                ws[hh] = w.astype(BF16)
            if 0 <= s - 2 < heads:
                hh = s - 2
                pv = jnp.dot(ws[hh], v_ref[pl.ds(ks, t), lanes[hh]], preferred_element_type=F32)
                block_sum = jnp.sum(log_remains[hh], axis=1, keepdims=True)
                if first:
                    acc_ref[hh] = pv
                    r_new = block_sum
                else:
                    r_old = r_ref[hh]
                    acc_ref[hh] += jnp.exp2(r_old) * pv
                    r_new = r_old + block_sum
                r_ref[hh] = r_new
                alive = r_new if alive is None else jnp.maximum(alive, r_new)
        return (jnp.max(alive) > LOG2_WEIGHT_FLOOR).astype(jnp.int32)

    go = step(i, True)

    def cond(c):
        n, go = c
        return jnp.logical_and(n < i, go > 0)

    def body(c):
        n, _ = c
        return n + 1, step(i - 1 - n, False)

    lax.while_loop(cond, body, (jnp.int32(0), go))
    for hh in range(heads):
        o_ref[:, lanes[hh]] = acc_ref[hh].astype(o_ref.dtype)


def _attention(proj3):
    b, s, _ = proj3.shape
    width = ATT_HEADS_PER_STEP * ATT_HEAD_DIM
    grid = (b, ATT_HEADS // ATT_HEADS_PER_STEP, s // ATT_TILE)
    qb, kb, vb = Q_OFF // width, K_OFF // width, V_OFF // width
    return pl.pallas_call(
        _attn_kernel,
        grid=grid,
        in_specs=[
            pl.BlockSpec((None, ATT_TILE, width), lambda bi, h, i: (bi, i, qb + h)),
            pl.BlockSpec((None, s, width), lambda bi, h, i: (bi, 0, kb + h)),
            pl.BlockSpec((None, s, width), lambda bi, h, i: (bi, 0, vb + h)),
        ],
        out_specs=pl.BlockSpec((None, ATT_TILE, width), lambda bi, h, i: (bi, i, h)),
        out_shape=jax.ShapeDtypeStruct((b, s, ATT_WIDTH), BF16),
        scratch_shapes=[
            pltpu.VMEM((ATT_HEADS_PER_STEP, ATT_TILE, 1), F32),
            pltpu.VMEM((ATT_HEADS_PER_STEP, ATT_TILE, ATT_HEAD_DIM), F32),
        ],
        compiler_params=_params("arbitrary", "arbitrary", "arbitrary"),
        name="attn",
    )(proj3, proj3, proj3)


def _split3(x):
    h1 = x.astype(BF16)
    r1 = x - h1.astype(F32)
    h2 = r1.astype(BF16)
    h3 = (r1 - h2.astype(F32)).astype(BF16)
    return h1, h2, h3


def _dot3(x, m, lhs=True):
    parts = _split3(x)
    if lhs:
        outs = [jnp.dot(p, m, preferred_element_type=F32) for p in parts]
    else:
        outs = [jnp.dot(m, p, preferred_element_type=F32) for p in parts]
    return outs[0] + outs[1] + outs[2]


def _silu(x):
    return x / (1.0 + jnp.exp2(x * (-LOG2_E)))


def _ssd_kernel(z_ref, x0_ref, x1_ref, bc_ref, p0_ref, p1_ref, pbc_ref, dt_ref, cw_ref, cb_ref, dtb_ref,
                alog_ref, dskip_ref, gain_ref, expand_ref, o_ref, state_ref):
    c = pl.program_id(1)
    L = SSD_CHUNK

    @pl.when(c == 0)
    def _():
        state_ref[...] = jnp.zeros_like(state_ref)

    out_t = lax.broadcasted_iota(jnp.int32, ((CONV_WIDTH - 1) * L, 2 * L), 0)
    src_t = lax.broadcasted_iota(jnp.int32, ((CONV_WIDTH - 1) * L, 2 * L), 1)
    shift = out_t // L + 1
    hit = src_t == L + (out_t - (shift - 1) * L) - shift
    shifts = jnp.logical_and(hit, jnp.logical_or(src_t >= L, c > 0)).astype(BF16)
    pieces = []
    col = 0
    for cur_ref, prev_ref in ((x0_ref, p0_ref), (x1_ref, p1_ref), (bc_ref, pbc_ref)):
        cur = cur_ref[...]
        width = cur.shape[1]
        both = jnp.concatenate([prev_ref[...], cur], axis=0)
        back = jnp.dot(shifts, both, preferred_element_type=F32)
        acc = cb_ref[:, col:col + width] + cur.astype(F32) * cw_ref[CONV_WIDTH - 1:CONV_WIDTH, col:col + width]
        for d in range(CONV_WIDTH - 1):
            tap = CONV_WIDTH - 2 - d
            acc = acc + back[d * L:(d + 1) * L] * cw_ref[tap:tap + 1, col:col + width]
        pieces.append(_silu(acc))
        col += width
    xs = jnp.concatenate(pieces[0:2], axis=1)
    b_in = pieces[2][:, 0:SSM_GROUPS * SSM_STATE]
    c_in = pieces[2][:, SSM_GROUPS * SSM_STATE:2 * SSM_GROUPS * SSM_STATE]

    lane = lax.broadcasted_iota(jnp.int32, (1, LANES), 1)
    a2 = jnp.where(lane < SSM_HEADS, -LOG2_E * jnp.exp(alog_ref[...]), 0.0)
    dt_pre = dt_ref[...] + dtb_ref[...]
    dt = jnp.maximum(dt_pre, 0.0) + jnp.log(1.0 + jnp.exp(-jnp.abs(dt_pre)))
    da = dt * a2
    rows = lax.broadcasted_iota(jnp.int32, (L, L), 0)
    cols = lax.broadcasted_iota(jnp.int32, (L, L), 1)
    causal = rows >= cols
    not_causal = jnp.where(causal, 0.0, -jnp.inf)
    a_cum = _dot3(da, causal.astype(BF16), lhs=False)
    src_t = (a_cum - jnp.log2(dt)).T
    a_last = a_cum[L - 1:L, :]
    expand = expand_ref[...]
    decay_in = jnp.dot(jnp.exp2(a_cum).astype(BF16), expand, preferred_element_type=F32)
    w_end = jnp.dot((jnp.exp2(a_last - a_cum) * dt).astype(BF16), expand, preferred_element_type=F32)
    chunk_decay = _dot3(jnp.broadcast_to(jnp.exp2(a_last), (SUBLANES, LANES)), expand)[0:1, :]

    xw = (xs * w_end).astype(BF16)
    lane_in_pair = lax.broadcasted_iota(jnp.int32, (L, LANES), 1)
    y_parts = []
    for g in range(SSM_GROUPS):
        bg = b_in[:, g * SSM_STATE:(g + 1) * SSM_STATE]
        cg = c_in[:, g * SSM_STATE:(g + 1) * SSM_STATE].astype(BF16)
        cb = lax.dot_general(cg, bg.astype(BF16), (((1,), (1,)), ((), ())),
                             preferred_element_type=F32)
        gcols = slice(g * GROUP_WIDTH, (g + 1) * GROUP_WIDTH)
        st = state_ref[g]
        y_off = jnp.dot(cg, st.astype(BF16), preferred_element_type=F32) * decay_in[:, gcols]
        st_new = jnp.dot(bg.T.astype(BF16), xw[:, gcols], preferred_element_type=F32)
        state_ref[g] = st * chunk_decay[:, gcols] + st_new
        heads_per_group = SSM_HEADS // SSM_GROUPS
        for pair in range(heads_per_group // 2):
            h0 = g * heads_per_group + 2 * pair
            slab = xs[:, h0 * SSM_HEAD_DIM:(h0 + 2) * SSM_HEAD_DIM]
            mats = []
            for hh in (h0, h0 + 1):
                seg = (a_cum[:, hh:hh + 1] - src_t[hh:hh + 1, :]) + not_causal
                mats.append((cb * jnp.exp2(seg)).astype(BF16))
            lhs = jnp.concatenate(mats, axis=1)
            rhs = jnp.concatenate(
                [jnp.where(lane_in_pair < SSM_HEAD_DIM, slab, 0.0),
                 jnp.where(lane_in_pair >= SSM_HEAD_DIM, slab, 0.0)], axis=0).astype(BF16)
            y_diag = jnp.dot(lhs, rhs, preferred_element_type=F32)
            lo = (h0 - g * heads_per_group) * SSM_HEAD_DIM
            y_parts.append(y_diag + y_off[:, lo:lo + LANES])
    y = jnp.concatenate(y_parts, axis=1) + xs * dskip_ref[...]
    y = y * _silu(z_ref[...].astype(F32))
    outs = []
    for g in range(SSM_GROUPS):
        yg = y[:, g * GROUP_WIDTH:(g + 1) * GROUP_WIDTH]
        ms = jnp.mean(yg * yg, axis=-1, keepdims=True)
        outs.append(yg * lax.rsqrt(ms + EPS))
    o_ref[...] = (jnp.concatenate(outs, axis=1) * gain_ref[...]).astype(o_ref.dtype)


def _ssd(proj3, dt3, conv_w, conv_b, dt_bias, a_log, d_skip_x, gain, expand):
    b, s, _ = proj3.shape
    L = SSD_CHUNK
    half = SSM_WIDTH // 2
    grid = (b, s // L)
    const = lambda shape: pl.BlockSpec(shape, lambda bi, c: (0, 0))
    xbc = lambda k: pl.BlockSpec((None, L, half), lambda bi, c: (bi, c, XBC_OFF // half + k))
    xbc_prev = lambda k: pl.BlockSpec((None, L, half),
                                      lambda bi, c: (bi, jnp.maximum(c - 1, 0), XBC_OFF // half + k))
    return pl.pallas_call(
        _ssd_kernel,
        grid=grid,
        in_specs=[
            pl.BlockSpec((None, L, SSM_WIDTH), lambda bi, c: (bi, c, Z_OFF // SSM_WIDTH)),
            xbc(0), xbc(1), xbc(2),
            xbc_prev(0), xbc_prev(1), xbc_prev(2),
            pl.BlockSpec((None, L, LANES), lambda bi, c: (bi, c, 0)),
            const((CONV_WIDTH, CONV_DIM)),
            const((1, CONV_DIM)),
            const((1, LANES)),
            const((1, LANES)),
            const((1, SSM_WIDTH)),
            const((1, SSM_WIDTH)),
            const((LANES, SSM_WIDTH)),
        ],
        out_specs=pl.BlockSpec((None, L, SSM_WIDTH), lambda bi, c: (bi, c, 0)),
        out_shape=jax.ShapeDtypeStruct((b, s, SSM_WIDTH), BF16),
        scratch_shapes=[pltpu.VMEM((SSM_GROUPS, SSM_STATE, GROUP_WIDTH), F32)],
        compiler_params=_params("arbitrary", "arbitrary"),
        name="ssd",
    )(proj3, proj3, proj3, proj3, proj3, proj3, proj3, dt3, conv_w, conv_b, dt_bias, a_log, d_skip_x,
      gain, expand)


def _outproj_kernel(x_ref, oa_ref, os_ref, g_ref, w_ref, o_ref):
    a = oa_ref[...].astype(F32)
    ms = jnp.mean(a * a, axis=-1, keepdims=True)
    an = (a * lax.rsqrt(ms + EPS) * g_ref[...]).astype(BF16)
    y = jnp.dot(an, w_ref[0:ATT_WIDTH, :], preferred_element_type=F32)
    y = y + jnp.dot(os_ref[...], w_ref[ATT_WIDTH:ATT_WIDTH + SSM_WIDTH, :], preferred_element_type=F32)
    o_ref[...] = x_ref[...] + y


def _out_proj(x2, o_att, o_ssm, gain, w_out, *, tm):
    t = x2.shape[0]
    return pl.pallas_call(
        _outproj_kernel,
        grid=(t // tm,),
        in_specs=[
            pl.BlockSpec((tm, D_MODEL), lambda i: (i, 0)),
            pl.BlockSpec((tm, ATT_WIDTH), lambda i: (i, 0)),
            pl.BlockSpec((tm, SSM_WIDTH), lambda i: (i, 0)),
            pl.BlockSpec((1, ATT_WIDTH), lambda i: (0, 0)),
            pl.BlockSpec((ATT_WIDTH + SSM_WIDTH, D_MODEL), lambda i: (0, 0)),
        ],
        out_specs=pl.BlockSpec((tm, D_MODEL), lambda i: (i, 0)),
        out_shape=jax.ShapeDtypeStruct((t, D_MODEL), F32),
        compiler_params=_params("arbitrary"),
        name="out_proj",
    )(x2, o_att, o_ssm, gain, w_out)


def _ffn_kernel(x_ref, g_ref, wg_ref, wu_ref, wd_ref, o_ref, h_ref):
    f = pl.program_id(1)

    @pl.when(f == 0)
    def _():
        x = x_ref[...]
        ms = jnp.mean(x * x, axis=-1, keepdims=True)
        h_ref[...] = (x * lax.rsqrt(ms + EPS) * g_ref[...]).astype(BF16)
        o_ref[...] = x

    h = h_ref[...]
    gate = jnp.dot(h, wg_ref[...], preferred_element_type=F32)
    up = jnp.dot(h, wu_ref[...], preferred_element_type=F32)
    act = (_silu(gate) * up).astype(BF16)
    o_ref[...] += jnp.dot(act, wd_ref[...], preferred_element_type=F32)


def _ffn(x2, g, w_gate, w_up, w_down, *, tm, tf):
    t = x2.shape[0]
    return pl.pallas_call(
        _ffn_kernel,
        grid=(t // tm, D_FF // tf),
        in_specs=[
            pl.BlockSpec((tm, D_MODEL), lambda i, f: (i, 0)),
            pl.BlockSpec((1, D_MODEL), lambda i, f: (0, 0)),
            pl.BlockSpec((D_MODEL, tf), lambda i, f: (0, f)),
            pl.BlockSpec((D_MODEL, tf), lambda i, f: (0, f)),
            pl.BlockSpec((tf, D_MODEL), lambda i, f: (f, 0)),
        ],
        out_specs=pl.BlockSpec((tm, D_MODEL), lambda i, f: (i, 0)),
        out_shape=jax.ShapeDtypeStruct((t, D_MODEL), F32),
        scratch_shapes=[pltpu.VMEM((tm, D_MODEL), BF16)],
        compiler_params=_params("arbitrary", "arbitrary"),
        name="ffn",
    )(x2, g, w_gate, w_up, w_down)


def _tile(pref, n):
    return pref if n % pref == 0 else n


def _layer(x, norm_mix, w_in, q_gain, k_gain, conv_w, conv_b, dt_bias, a_log, d_skip,
           attn_out_gain, ssm_out_gain, w_out, norm_ffn, w_gate, w_up, w_down, expand):
    b, s, _ = x.shape
    t = b * s
    x2 = x.reshape(t, D_MODEL)
    row = lambda v: v.reshape(1, -1).astype(F32)
    pad_lanes = lambda v: jnp.pad(v.astype(F32), (0, LANES - v.shape[0])).reshape(1, LANES)

    w_main = w_in[:, :MAIN_DIM].astype(BF16)
    w_dt = jnp.pad(w_in[:, MAIN_DIM:], ((0, 0), (0, LANES - SSM_HEADS))).astype(BF16)
    proj, dt_raw = _in_proj(x2, row(norm_mix), w_main, w_dt,
                            row(q_gain) * (LOG2_E * ATT_HEAD_DIM ** -0.5), row(k_gain),
                            tm=_tile(512, t), tn=MAIN_DIM // 2)
    proj3 = proj.reshape(b, s, MAIN_DIM)

    o_att = _attention(proj3)
    o_ssm = _ssd(proj3, dt_raw.reshape(b, s, LANES), conv_w.astype(F32), row(conv_b),
                 pad_lanes(dt_bias), pad_lanes(a_log),
                 row(jnp.repeat(d_skip, SSM_HEAD_DIM)), row(ssm_out_gain), expand)

    x2 = _out_proj(x2, o_att.reshape(t, ATT_WIDTH), o_ssm.reshape(t, SSM_WIDTH),
                   row(attn_out_gain), w_out.astype(BF16), tm=_tile(512, t))
    x2 = _ffn(x2, row(norm_ffn), w_gate.astype(BF16), w_up.astype(BF16), w_down.astype(BF16),
              tm=_tile(1024, t), tf=512)
    return x2.reshape(b, s, D_MODEL)


def kernel(x, norm_mix, w_in, q_gain, k_gain, conv_w, conv_b, dt_bias, a_log, d_skip, attn_out_gain, ssm_out_gain, w_out, norm_ffn, w_gate, w_up, w_down):
    head_of_lane = jnp.arange(SSM_WIDTH, dtype=jnp.int32) // SSM_HEAD_DIM
    expand = (jnp.arange(LANES, dtype=jnp.int32)[:, None] == head_of_lane[None, :]).astype(BF16)
    for i in range(norm_mix.shape[0]):
        x = _layer(x, norm_mix[i], w_in[i], q_gain[i], k_gain[i], conv_w[i], conv_b[i],
                   dt_bias[i], a_log[i], d_skip[i], attn_out_gain[i], ssm_out_gain[i],
                   w_out[i], norm_ffn[i], w_gate[i], w_up[i], w_down[i], expand)
    return x
```

```python
import functools

import jax
import jax.numpy as jnp
from jax import lax
from jax.experimental import pallas as pl
from jax.experimental.pallas import tpu as pltpu

F32 = jnp.float32
BF16 = jnp.bfloat16

D_MODEL = 2048
ATT_WIDTH = 1024
SSM_WIDTH = 1024
ATT_HEAD_DIM = 128
ATT_HEADS = ATT_WIDTH // ATT_HEAD_DIM
SSM_HEAD_DIM = 64
SSM_HEADS = SSM_WIDTH // SSM_HEAD_DIM
SSM_GROUPS = 2
SSM_STATE = 128
GROUP_WIDTH = SSM_WIDTH // SSM_GROUPS
CONV_WIDTH = 4
CONV_DIM = SSM_WIDTH + 2 * SSM_GROUPS * SSM_STATE
SSD_CHUNK = 128
D_FF = 5632
EPS = 1e-6
LOG2_E = 1.4426950408889634

LANES = 128
SUBLANES = 8
VMEM_LIMIT = 56 * 1024 * 1024

MAIN_DIM = 3 * ATT_WIDTH + SSM_WIDTH + CONV_DIM
Q_OFF, K_OFF, V_OFF = 0, ATT_WIDTH, 2 * ATT_WIDTH
Z_OFF = 3 * ATT_WIDTH
XBC_OFF = Z_OFF + SSM_WIDTH


def _params(*sem):
    return pltpu.CompilerParams(dimension_semantics=sem, vmem_limit_bytes=VMEM_LIMIT)


INPROJ_SUB = 512

def _inproj_kernel(x_ref, g_ref, w_ref, wdt_ref, qg_ref, kg_ref, o_ref, dt_ref, h_ref, *, tn):
    j = pl.program_id(1)
    assert tn >= K_OFF + ATT_WIDTH

    def columns(first_tile):
        h = h_ref[...]
        for lo in range(0, tn, INPROJ_SUB):
            hi = min(lo + INPROJ_SUB, tn)
            acc = jnp.dot(h, w_ref[:, lo:hi], preferred_element_type=F32)
            for c0 in range(lo, hi, ATT_HEAD_DIM):
                seg = acc[:, c0 - lo:c0 - lo + ATT_HEAD_DIM]
                if first_tile and c0 < K_OFF + ATT_WIDTH:
                    gain = qg_ref[...] if c0 < K_OFF else kg_ref[...]
                    ms = jnp.mean(seg * seg, axis=-1, keepdims=True)
                    seg = seg * lax.rsqrt(ms + EPS) * gain
                o_ref[:, c0:c0 + ATT_HEAD_DIM] = seg.astype(BF16)

    @pl.when(j == 0)
    def _():
        x = x_ref[...]
        ms = jnp.mean(x * x, axis=-1, keepdims=True)
        h = (x * lax.rsqrt(ms + EPS) * g_ref[...]).astype(BF16)
        h_ref[...] = h
        dt_ref[...] = jnp.dot(h, wdt_ref[...], preferred_element_type=F32)
        columns(True)

    @pl.when(j > 0)
    def _():
        columns(False)


def _in_proj(x2, g, w_main, w_dt, layer, qg, kg, *, tm, tn):
    t = x2.shape[0]
    grid = (t // tm, MAIN_DIM // tn)
    return pl.pallas_call(
        functools.partial(_inproj_kernel, tn=tn),
        grid=grid,
        in_specs=[
            pl.BlockSpec((tm, D_MODEL), lambda i, j: (i, 0)),
            pl.BlockSpec((1, D_MODEL), lambda i, j: (0, 0)),
            pl.BlockSpec((None, D_MODEL, tn), lambda i, j: (layer, 0, j)),
            pl.BlockSpec((None, D_MODEL, LANES), lambda i, j: (layer, 0, 0)),
            pl.BlockSpec((1, ATT_HEAD_DIM), lambda i, j: (0, 0)),
            pl.BlockSpec((1, ATT_HEAD_DIM), lambda i, j: (0, 0)),
        ],
        out_specs=[
            pl.BlockSpec((tm, tn), lambda i, j: (i, j)),
            pl.BlockSpec((tm, LANES), lambda i, j: (i, 0)),
        ],
        out_shape=[
            jax.ShapeDtypeStruct((t, MAIN_DIM), BF16),
            jax.ShapeDtypeStruct((t, LANES), F32),
        ],
        scratch_shapes=[pltpu.VMEM((tm, D_MODEL), BF16)],
        compiler_params=_params("arbitrary", "arbitrary"),
        name="in_proj",
    )(x2, g, w_main, w_dt, qg, kg)


ATT_TILE = 256
ATT_HEADS_PER_STEP = 8
LOG2_WEIGHT_FLOOR = -152.0


def _attn_kernel(q_ref, k_ref, v_ref, o_ref, r_ref, acc_ref):
    t = ATT_TILE
    heads = ATT_HEADS_PER_STEP
    i = pl.program_id(2)
    rows = lax.broadcasted_iota(jnp.int32, (t, t), 0)
    cols = lax.broadcasted_iota(jnp.int32, (t, t), 1)
    later_sum = (rows > cols).astype(BF16)
    causal = cols < rows
    sign_bit = jnp.uint32(0x80000000)
    lanes = [slice(hh * ATT_HEAD_DIM, (hh + 1) * ATT_HEAD_DIM) for hh in range(heads)]

    def step(jblk, first):
        ks = pl.multiple_of(jblk * t, t)
        log_betas, log_remains, ws = [None] * heads, [None] * heads, [None] * heads
        alive = None
        for s in range(heads + 2):
            if s < heads:
                z = lax.dot_general(q_ref[:, lanes[s]], k_ref[pl.ds(ks, t), lanes[s]],
                                    (((1,), (1,)), ((), ())), preferred_element_type=F32)
                neg_abs = lax.bitcast_convert_type(lax.bitcast_convert_type(z, jnp.uint32) | sign_bit, F32)
                log_beta = jnp.minimum(z, 0.0) - jnp.log2(1.0 + jnp.exp2(neg_abs))
                log_remain = log_beta - z
                if first:
                    log_remain = jnp.where(causal, log_remain, 0.0)
                log_betas[s], log_remains[s] = log_beta, log_remain
            if 0 <= s - 1 < heads:
                hh = s - 1
                within = jnp.dot(log_remains[hh].astype(BF16), later_sum, preferred_element_type=F32)
                w = jnp.exp2(log_betas[hh] + within)
                if first:
                    w = jnp.where(causal, w, 0.0)
                ws[hh] = w.astype(BF16)
            if 0 <= s - 2 < heads:
                hh = s - 2
                pv = jnp.dot(ws[hh], v_ref[pl.ds(ks, t), lanes[hh]], preferred_element_type=F32)
                block_sum = jnp.sum(log_remains[hh], axis=1, keepdims=True)
                if first:
                    acc_ref[hh] = pv
                    r_new = block_sum
                else:
                    r_old = r_ref[hh]
                    acc_ref[hh] += jnp.exp2(r_old) * pv
                    r_new = r_old + block_sum
                r_ref[hh] = r_new
                alive = r_new if alive is None else jnp.maximum(alive, r_new)
        return (jnp.max(alive) > LOG2_WEIGHT_FLOOR).astype(jnp.int32)

    go = step(i, True)

    def cond(c):
        n, go = c
        return jnp.logical_and(n < i, go > 0)

    def body(c):
        n, _ = c
        return n + 1, step(i - 1 - n, False)

    lax.while_loop(cond, body, (jnp.int32(0), go))
    for hh in range(heads):
        o_ref[:, lanes[hh]] = acc_ref[hh].astype(o_ref.dtype)


def _attention(proj3):
    b, s, _ = proj3.shape
    width = ATT_HEADS_PER_STEP * ATT_HEAD_DIM
    grid = (b, ATT_HEADS // ATT_HEADS_PER_STEP, s // ATT_TILE)
    qb, kb, vb = Q_OFF // width, K_OFF // width, V_OFF // width
    return pl.pallas_call(
        _attn_kernel,
        grid=grid,
        in_specs=[
            pl.BlockSpec((None, ATT_TILE, width), lambda bi, h, i: (bi, i, qb + h)),
            pl.BlockSpec((None, s, width), lambda bi, h, i: (bi, 0, kb + h)),
            pl.BlockSpec((None, s, width), lambda bi, h, i: (bi, 0, vb + h)),
        ],
        out_specs=pl.BlockSpec((None, ATT_TILE, width), lambda bi, h, i: (bi, i, h)),
        out_shape=jax.ShapeDtypeStruct((b, s, ATT_WIDTH), BF16),
        scratch_shapes=[
            pltpu.VMEM((ATT_HEADS_PER_STEP, ATT_TILE, 1), F32),
            pltpu.VMEM((ATT_HEADS_PER_STEP, ATT_TILE, ATT_HEAD_DIM), F32),
        ],
        compiler_params=_params("arbitrary", "arbitrary", "arbitrary"),
        name="attn",
    )(proj3, proj3, proj3)


def _split3(x):
    h1 = x.astype(BF16)
    r1 = x - h1.astype(F32)
    h2 = r1.astype(BF16)
    h3 = (r1 - h2.astype(F32)).astype(BF16)
    return h1, h2, h3


def _dot3(x, m, lhs=True):
    parts = _split3(x)
    if lhs:
        outs = [jnp.dot(p, m, preferred_element_type=F32) for p in parts]
    else:
        outs = [jnp.dot(m, p, preferred_element_type=F32) for p in parts]
    return outs[0] + outs[1] + outs[2]


def _silu(x):
    return x / (1.0 + jnp.exp2(x * (-LOG2_E)))


SSD_CHUNKS_PER_STEP = 2


def _ssd_chunk(first, z, cur, prev, dt_raw, cw_ref, cb_ref, dtb_ref, alog_ref, dskip_ref, gain_ref,
               expand_ref, state_ref):
    L = SSD_CHUNK
    out_t = lax.broadcasted_iota(jnp.int32, ((CONV_WIDTH - 1) * L, 2 * L), 0)
    src_t = lax.broadcasted_iota(jnp.int32, ((CONV_WIDTH - 1) * L, 2 * L), 1)
    shift = out_t // L + 1
    hit = src_t == L + (out_t - (shift - 1) * L) - shift
    shifts = jnp.logical_and(hit, jnp.logical_or(src_t >= L, jnp.logical_not(first))).astype(BF16)
    pieces = []
    col = 0
    for cur_blk, prev_blk in zip(cur, prev):
        width = cur_blk.shape[1]
        both = jnp.concatenate([prev_blk, cur_blk], axis=0)
        back = jnp.dot(shifts, both, preferred_element_type=F32)
        acc = cb_ref[:, col:col + width] + cur_blk.astype(F32) * cw_ref[CONV_WIDTH - 1:CONV_WIDTH, col:col + width]
        for d in range(CONV_WIDTH - 1):
            tap = CONV_WIDTH - 2 - d
            acc = acc + back[d * L:(d + 1) * L] * cw_ref[tap:tap + 1, col:col + width]
        pieces.append(_silu(acc))
        col += width
    xs = jnp.concatenate(pieces[0:2], axis=1)
    b_in = pieces[2][:, 0:SSM_GROUPS * SSM_STATE]
    c_in = pieces[2][:, SSM_GROUPS * SSM_STATE:2 * SSM_GROUPS * SSM_STATE]

    lane = lax.broadcasted_iota(jnp.int32, (1, LANES), 1)
    a2 = jnp.where(lane < SSM_HEADS, -LOG2_E * jnp.exp(alog_ref[...]), 0.0)
    dt_pre = dt_raw + dtb_ref[...]
    dt = jnp.maximum(dt_pre, 0.0) + jnp.log(1.0 + jnp.exp(-jnp.abs(dt_pre)))
    da = dt * a2
    rows = lax.broadcasted_iota(jnp.int32, (L, L), 0)
    cols = lax.broadcasted_iota(jnp.int32, (L, L), 1)
    causal = rows >= cols
    not_causal = jnp.where(causal, 0.0, -jnp.inf)
    a_cum = _dot3(da, causal.astype(BF16), lhs=False)
    src_side = (a_cum - jnp.log2(dt)).T
    a_last = a_cum[L - 1:L, :]
    expand = expand_ref[...]
    decay_in = jnp.dot(jnp.exp2(a_cum).astype(BF16), expand, preferred_element_type=F32)
    w_end = jnp.dot((jnp.exp2(a_last - a_cum) * dt).astype(BF16), expand, preferred_element_type=F32)
    chunk_decay = _dot3(jnp.broadcast_to(jnp.exp2(a_last), (SUBLANES, LANES)), expand)[0:1, :]

    xw = (xs * w_end).astype(BF16)
    lane_in_pair = lax.broadcasted_iota(jnp.int32, (L, LANES), 1)
    y_parts = []
    for g in range(SSM_GROUPS):
        bg = b_in[:, g * SSM_STATE:(g + 1) * SSM_STATE]
        cg = c_in[:, g * SSM_STATE:(g + 1) * SSM_STATE].astype(BF16)
        cb = lax.dot_general(cg, bg.astype(BF16), (((1,), (1,)), ((), ())),
                             preferred_element_type=F32)
        gcols = slice(g * GROUP_WIDTH, (g + 1) * GROUP_WIDTH)
        st = state_ref[g]
        y_off = jnp.dot(cg, st.astype(BF16), preferred_element_type=F32) * decay_in[:, gcols]
        st_new = jnp.dot(bg.T.astype(BF16), xw[:, gcols], preferred_element_type=F32)
        state_ref[g] = st * chunk_decay[:, gcols] + st_new
        heads_per_group = SSM_HEADS // SSM_GROUPS
        for pair in range(heads_per_group // 2):
            h0 = g * heads_per_group + 2 * pair
            slab = xs[:, h0 * SSM_HEAD_DIM:(h0 + 2) * SSM_HEAD_DIM]
            mats = []
            for hh in (h0, h0 + 1):
                seg = (a_cum[:, hh:hh + 1] - src_side[hh:hh + 1, :]) + not_causal
                mats.append((cb * jnp.exp2(seg)).astype(BF16))
            lhs = jnp.concatenate(mats, axis=1)
            rhs = jnp.concatenate(
                [jnp.where(lane_in_pair < SSM_HEAD_DIM, slab, 0.0),
                 jnp.where(lane_in_pair >= SSM_HEAD_DIM, slab, 0.0)], axis=0).astype(BF16)
            y_diag = jnp.dot(lhs, rhs, preferred_element_type=F32)
            lo = (h0 - g * heads_per_group) * SSM_HEAD_DIM
            y_parts.append(y_diag + y_off[:, lo:lo + LANES])
    y = jnp.concatenate(y_parts, axis=1) + xs * dskip_ref[...]
    y = y * _silu(z.astype(F32))
    outs = []
    for g in range(SSM_GROUPS):
        yg = y[:, g * GROUP_WIDTH:(g + 1) * GROUP_WIDTH]
        ms = jnp.mean(yg * yg, axis=-1, keepdims=True)
        outs.append(yg * lax.rsqrt(ms + EPS))
    return (jnp.concatenate(outs, axis=1) * gain_ref[...]).astype(BF16)


def _ssd_kernel(z_ref, x0_ref, x1_ref, bc_ref, p0_ref, p1_ref, pbc_ref, dt_ref, cw_ref, cb_ref, dtb_ref,
                alog_ref, dskip_ref, gain_ref, expand_ref, o_ref, state_ref):
    c = pl.program_id(1)
    L = SSD_CHUNK

    @pl.when(c == 0)
    def _():
        state_ref[...] = jnp.zeros_like(state_ref)

    cur_refs = (x0_ref, x1_ref, bc_ref)
    prev = tuple(r[...] for r in (p0_ref, p1_ref, pbc_ref))
    first = c == 0
    for k in range(SSD_CHUNKS_PER_STEP):
        rows = slice(k * L, (k + 1) * L)
        cur = tuple(r[rows, :] for r in cur_refs)
        o_ref[rows, :] = _ssd_chunk(first, z_ref[rows, :], cur, prev, dt_ref[rows, :], cw_ref, cb_ref, dtb_ref,
                                    alog_ref, dskip_ref, gain_ref, expand_ref, state_ref)
        prev = cur
        first = jnp.bool_(False)


def _ssd(proj3, dt3, conv_w, conv_b, dt_bias, a_log, d_skip_x, gain, expand):
    b, s, _ = proj3.shape
    L = SSD_CHUNK
    per_step = SSD_CHUNKS_PER_STEP if s % (SSD_CHUNKS_PER_STEP * L) == 0 else 1
    assert per_step == SSD_CHUNKS_PER_STEP
    rows = per_step * L
    half = SSM_WIDTH // 2
    grid = (b, s // rows)
    const = lambda shape: pl.BlockSpec(shape, lambda bi, c: (0, 0))
    xbc = lambda k: pl.BlockSpec((None, rows, half), lambda bi, c: (bi, c, XBC_OFF // half + k))
    xbc_prev = lambda k: pl.BlockSpec(
        (None, L, half), lambda bi, c: (bi, jnp.maximum(c * per_step - 1, 0), XBC_OFF // half + k))
    return pl.pallas_call(
        _ssd_kernel,
        grid=grid,
        in_specs=[
            pl.BlockSpec((None, rows, SSM_WIDTH), lambda bi, c: (bi, c, Z_OFF // SSM_WIDTH)),
            xbc(0), xbc(1), xbc(2),
            xbc_prev(0), xbc_prev(1), xbc_prev(2),
            pl.BlockSpec((None, rows, LANES), lambda bi, c: (bi, c, 0)),
            const((CONV_WIDTH, CONV_DIM)),
            const((1, CONV_DIM)),
            const((1, LANES)),
            const((1, LANES)),
            const((1, SSM_WIDTH)),
            const((1, SSM_WIDTH)),
            const((LANES, SSM_WIDTH)),
        ],
        out_specs=pl.BlockSpec((None, rows, SSM_WIDTH), lambda bi, c: (bi, c, 0)),
        out_shape=jax.ShapeDtypeStruct((b, s, SSM_WIDTH), BF16),
        scratch_shapes=[pltpu.VMEM((SSM_GROUPS, SSM_STATE, GROUP_WIDTH), F32)],
        compiler_params=_params("arbitrary", "arbitrary"),
        name="ssd",
    )(proj3, proj3, proj3, proj3, proj3, proj3, proj3, dt3, conv_w, conv_b, dt_bias, a_log, d_skip_x,
      gain, expand)


def _outproj_kernel(x_ref, oa_ref, os_ref, g_ref, w_ref, o_ref):
    a = oa_ref[...].astype(F32)
    ms = jnp.mean(a * a, axis=-1, keepdims=True)
    an = (a * lax.rsqrt(ms + EPS) * g_ref[...]).astype(BF16)
    y = jnp.dot(an, w_ref[0:ATT_WIDTH, :], preferred_element_type=F32)
    y = y + jnp.dot(os_ref[...], w_ref[ATT_WIDTH:ATT_WIDTH + SSM_WIDTH, :], preferred_element_type=F32)
    o_ref[...] = x_ref[...] + y


def _out_proj(x2, o_att, o_ssm, gain, w_out, layer, *, tm):
    t = x2.shape[0]
    return pl.pallas_call(
        _outproj_kernel,
        grid=(t // tm,),
        in_specs=[
            pl.BlockSpec((tm, D_MODEL), lambda i: (i, 0)),
            pl.BlockSpec((tm, ATT_WIDTH), lambda i: (i, 0)),
            pl.BlockSpec((tm, SSM_WIDTH), lambda i: (i, 0)),
            pl.BlockSpec((1, ATT_WIDTH), lambda i: (0, 0)),
            pl.BlockSpec((None, ATT_WIDTH + SSM_WIDTH, D_MODEL), lambda i: (layer, 0, 0)),
        ],
        out_specs=pl.BlockSpec((tm, D_MODEL), lambda i: (i, 0)),
        out_shape=jax.ShapeDtypeStruct((t, D_MODEL), F32),
        compiler_params=_params("arbitrary"),
        name="out_proj",
    )(x2, o_att, o_ssm, gain, w_out)


def _ffn_kernel(x_ref, g_ref, wg_ref, wu_ref, wd_ref, o_ref, h_ref):
    f = pl.program_id(1)

    @pl.when(f == 0)
    def _():
        x = x_ref[...]
        ms = jnp.mean(x * x, axis=-1, keepdims=True)
        h_ref[...] = (x * lax.rsqrt(ms + EPS) * g_ref[...]).astype(BF16)
        o_ref[...] = x

    h = h_ref[...]
    gate = jnp.dot(h, wg_ref[...], preferred_element_type=F32)
    up = jnp.dot(h, wu_ref[...], preferred_element_type=F32)
    act = (_silu(gate) * up).astype(BF16)
    o_ref[...] += jnp.dot(act, wd_ref[...], preferred_element_type=F32)


def _ffn(x2, g, w_gate, w_up, w_down, layer, *, tm, tf):
    t = x2.shape[0]
    return pl.pallas_call(
        _ffn_kernel,
        grid=(t // tm, D_FF // tf),
        in_specs=[
            pl.BlockSpec((tm, D_MODEL), lambda i, f: (i, 0)),
            pl.BlockSpec((1, D_MODEL), lambda i, f: (0, 0)),
            pl.BlockSpec((None, D_MODEL, tf), lambda i, f: (layer, 0, f)),
            pl.BlockSpec((None, D_MODEL, tf), lambda i, f: (layer, 0, f)),
            pl.BlockSpec((None, tf, D_MODEL), lambda i, f: (layer, f, 0)),
        ],
        out_specs=pl.BlockSpec((tm, D_MODEL), lambda i, f: (i, 0)),
        out_shape=jax.ShapeDtypeStruct((t, D_MODEL), F32),
        scratch_shapes=[pltpu.VMEM((tm, D_MODEL), BF16)],
        compiler_params=_params("arbitrary", "arbitrary"),
        name="ffn",
    )(x2, g, w_gate, w_up, w_down)


def _tile(pref, n):
    return pref if n % pref == 0 else n


def _layer(x, layer, norm_mix, q_gain, k_gain, conv_w, conv_b, dt_bias, a_log, d_skip,
           attn_out_gain, ssm_out_gain, norm_ffn, w_main, w_dt, w_out, w_gate, w_up, w_down, expand):
    b, s, _ = x.shape
    t = b * s
    x2 = x.reshape(t, D_MODEL)
    row = lambda v: v.reshape(1, -1).astype(F32)
    pad_lanes = lambda v: jnp.pad(v.astype(F32), (0, LANES - v.shape[0])).reshape(1, LANES)

    proj, dt_raw = _in_proj(x2, row(norm_mix), w_main, w_dt, layer,
                            row(q_gain) * (LOG2_E * ATT_HEAD_DIM ** -0.5), row(k_gain),
                            tm=_tile(512, t), tn=MAIN_DIM // 2)
    proj3 = proj.reshape(b, s, MAIN_DIM)

    o_att = _attention(proj3)
    o_ssm = _ssd(proj3, dt_raw.reshape(b, s, LANES), conv_w.astype(F32), row(conv_b),
                 pad_lanes(dt_bias), pad_lanes(a_log),
                 row(jnp.repeat(d_skip, SSM_HEAD_DIM)), row(ssm_out_gain), expand)

    x2 = _out_proj(x2, o_att.reshape(t, ATT_WIDTH), o_ssm.reshape(t, SSM_WIDTH),
                   row(attn_out_gain), w_out, layer, tm=_tile(512, t))
    x2 = _ffn(x2, row(norm_ffn), w_gate, w_up, w_down, layer, tm=_tile(1024, t), tf=512)
    return x2.reshape(b, s, D_MODEL)


def _bf16_weights(w_in, w_out, w_gate, w_up, w_down):
    w_main = w_in[:, :, :MAIN_DIM].astype(BF16)
    w_dt = jnp.pad(w_in[:, :, MAIN_DIM:], ((0, 0), (0, 0), (0, LANES - SSM_HEADS))).astype(BF16)
    return w_main, w_dt, w_out.astype(BF16), w_gate.astype(BF16), w_up.astype(BF16), w_down.astype(BF16)


def _head_expand():
    head_of_lane = jnp.arange(SSM_WIDTH, dtype=jnp.int32) // SSM_HEAD_DIM
    return (jnp.arange(LANES, dtype=jnp.int32)[:, None] == head_of_lane[None, :]).astype(BF16)


def kernel(x, norm_mix, w_in, q_gain, k_gain, conv_w, conv_b, dt_bias, a_log, d_skip, attn_out_gain, ssm_out_gain, w_out, norm_ffn, w_gate, w_up, w_down):
    weights = _bf16_weights(w_in, w_out, w_gate, w_up, w_down)
    expand = _head_expand()
    for i in range(norm_mix.shape[0]):
        x = _layer(x, i, norm_mix[i], q_gain[i], k_gain[i], conv_w[i], conv_b[i], dt_bias[i], a_log[i],
                   d_skip[i], attn_out_gain[i], ssm_out_gain[i], norm_ffn[i], *weights, expand)
    return x
```

```python
import functools

import jax
import jax.numpy as jnp
from jax import lax
from jax.experimental import pallas as pl
from jax.experimental.pallas import tpu as pltpu

F32 = jnp.float32
BF16 = jnp.bfloat16

D_MODEL = 2048
ATT_WIDTH = 1024
SSM_WIDTH = 1024
ATT_HEAD_DIM = 128
ATT_HEADS = ATT_WIDTH // ATT_HEAD_DIM
SSM_HEAD_DIM = 64
SSM_HEADS = SSM_WIDTH // SSM_HEAD_DIM
SSM_GROUPS = 2
SSM_STATE = 128
GROUP_WIDTH = SSM_WIDTH // SSM_GROUPS
CONV_WIDTH = 4
CONV_DIM = SSM_WIDTH + 2 * SSM_GROUPS * SSM_STATE
SSD_CHUNK = 128
D_FF = 5632
EPS = 1e-6
LOG2_E = 1.4426950408889634

LANES = 128
SUBLANES = 8
VMEM_LIMIT = 56 * 1024 * 1024

MAIN_DIM = 3 * ATT_WIDTH + SSM_WIDTH + CONV_DIM
Q_OFF, K_OFF, V_OFF = 0, ATT_WIDTH, 2 * ATT_WIDTH
Z_OFF = 3 * ATT_WIDTH
XBC_OFF = Z_OFF + SSM_WIDTH


def _params(*sem):
    return pltpu.CompilerParams(dimension_semantics=sem, vmem_limit_bytes=VMEM_LIMIT)


def _silu(x):
    return x / (1.0 + jnp.exp2(x * (-LOG2_E)))


INPROJ_SUB = 512


def _inproj_kernel(x_ref, g_ref, w_ref, wdt_ref, qg_ref, kg_ref, o_ref, dt_ref, h_ref, *, tn):
    j = pl.program_id(1)
    assert tn >= K_OFF + ATT_WIDTH

    def columns(first_tile):
        h = h_ref[...]
        for lo in range(0, tn, INPROJ_SUB):
            hi = min(lo + INPROJ_SUB, tn)
            acc = jnp.dot(h, w_ref[:, lo:hi], preferred_element_type=F32)
            for c0 in range(lo, hi, ATT_HEAD_DIM):
                seg = acc[:, c0 - lo:c0 - lo + ATT_HEAD_DIM]
                if first_tile and c0 < K_OFF + ATT_WIDTH:
                    gain = qg_ref[...] if c0 < K_OFF else kg_ref[...]
                    ms = jnp.mean(seg * seg, axis=-1, keepdims=True)
                    seg = seg * lax.rsqrt(ms + EPS) * gain
                o_ref[:, c0:c0 + ATT_HEAD_DIM] = seg.astype(BF16)

    @pl.when(j == 0)
    def _():
        x = x_ref[...]
        ms = jnp.mean(x * x, axis=-1, keepdims=True)
        h = (x * lax.rsqrt(ms + EPS) * g_ref[...]).astype(BF16)
        h_ref[...] = h
        dt_ref[...] = jnp.dot(h, wdt_ref[...], preferred_element_type=F32)
        columns(True)

    @pl.when(j > 0)
    def _():
        columns(False)


def _in_proj(x2, g, w_main, w_dt, layer, qg, kg, *, tm, tn):
    t = x2.shape[0]
    grid = (t // tm, MAIN_DIM // tn)
    return pl.pallas_call(
        functools.partial(_inproj_kernel, tn=tn),
        grid=grid,
        in_specs=[
            pl.BlockSpec((tm, D_MODEL), lambda i, j: (i, 0)),
            pl.BlockSpec((1, D_MODEL), lambda i, j: (0, 0)),
            pl.BlockSpec((None, D_MODEL, tn), lambda i, j: (layer, 0, j)),
            pl.BlockSpec((None, D_MODEL, LANES), lambda i, j: (layer, 0, 0)),
            pl.BlockSpec((1, ATT_HEAD_DIM), lambda i, j: (0, 0)),
            pl.BlockSpec((1, ATT_HEAD_DIM), lambda i, j: (0, 0)),
        ],
        out_specs=[
            pl.BlockSpec((tm, tn), lambda i, j: (i, j)),
            pl.BlockSpec((tm, LANES), lambda i, j: (i, 0)),
        ],
        out_shape=[
            jax.ShapeDtypeStruct((t, MAIN_DIM), BF16),
            jax.ShapeDtypeStruct((t, LANES), F32),
        ],
        scratch_shapes=[pltpu.VMEM((tm, D_MODEL), BF16)],
        compiler_params=_params("arbitrary", "arbitrary"),
        name="in_proj",
    )(x2, g, w_main, w_dt, qg, kg)


ATT_TILE = 256
ATT_HEADS_PER_STEP = 8
LOG2_WEIGHT_FLOOR = -152.0


def _attn_kernel(q_ref, k_ref, v_ref, o_ref, r_ref, acc_ref):
    t = ATT_TILE
    heads = ATT_HEADS_PER_STEP
    i = pl.program_id(2)
    rows = lax.broadcasted_iota(jnp.int32, (t, t), 0)
    cols = lax.broadcasted_iota(jnp.int32, (t, t), 1)
    later_sum = (rows > cols).astype(BF16)
    causal = cols < rows
    sign_bit = jnp.uint32(0x80000000)
    lanes = [slice(hh * ATT_HEAD_DIM, (hh + 1) * ATT_HEAD_DIM) for hh in range(heads)]

    def step(jblk, first):
        ks = pl.multiple_of(jblk * t, t)
        log_betas, log_remains, ws = [None] * heads, [None] * heads, [None] * heads
        alive = None
        for s in range(heads + 2):
            if s < heads:
                z = lax.dot_general(q_ref[:, lanes[s]], k_ref[pl.ds(ks, t), lanes[s]],
                                    (((1,), (1,)), ((), ())), preferred_element_type=F32)
                neg_abs = lax.bitcast_convert_type(lax.bitcast_convert_type(z, jnp.uint32) | sign_bit, F32)
                log_beta = jnp.minimum(z, 0.0) - jnp.log2(1.0 + jnp.exp2(neg_abs))
                log_remain = log_beta - z
                if first:
                    log_remain = jnp.where(causal, log_remain, 0.0)
                log_betas[s], log_remains[s] = log_beta, log_remain
            if 0 <= s - 1 < heads:
                hh = s - 1
                within = jnp.dot(log_remains[hh].astype(BF16), later_sum, preferred_element_type=F32)
                w = jnp.exp2(log_betas[hh] + within)
                if first:
                    w = jnp.where(causal, w, 0.0)
                ws[hh] = w.astype(BF16)
            if 0 <= s - 2 < heads:
                hh = s - 2
                pv = jnp.dot(ws[hh], v_ref[pl.ds(ks, t), lanes[hh]], preferred_element_type=F32)
                block_sum = jnp.sum(log_remains[hh], axis=1, keepdims=True)
                if first:
                    acc_ref[hh] = pv
                    r_new = block_sum
                else:
                    r_old = r_ref[hh]
                    acc_ref[hh] += jnp.exp2(r_old) * pv
                    r_new = r_old + block_sum
                r_ref[hh] = r_new
                alive = r_new if alive is None else jnp.maximum(alive, r_new)
        return (jnp.max(alive) > LOG2_WEIGHT_FLOOR).astype(jnp.int32)

    go = step(i, True)

    def cond(c):
        n, go = c
        return jnp.logical_and(n < i, go > 0)

    def body(c):
        n, _ = c
        return n + 1, step(i - 1 - n, False)

    lax.while_loop(cond, body, (jnp.int32(0), go))
    for hh in range(heads):
        o_ref[:, lanes[hh]] = acc_ref[hh].astype(o_ref.dtype)


def _attention(proj3):
    b, s, _ = proj3.shape
    width = ATT_HEADS_PER_STEP * ATT_HEAD_DIM
    grid = (b, ATT_HEADS // ATT_HEADS_PER_STEP, s // ATT_TILE)
    qb, kb, vb = Q_OFF // width, K_OFF // width, V_OFF // width
    return pl.pallas_call(
        _attn_kernel,
        grid=grid,
        in_specs=[
            pl.BlockSpec((None, ATT_TILE, width), lambda bi, h, i: (bi, i, qb + h)),
            pl.BlockSpec((None, s, width), lambda bi, h, i: (bi, 0, kb + h)),
            pl.BlockSpec((None, s, width), lambda bi, h, i: (bi, 0, vb + h)),
        ],
        out_specs=pl.BlockSpec((None, ATT_TILE, width), lambda bi, h, i: (bi, i, h)),
        out_shape=jax.ShapeDtypeStruct((b, s, ATT_WIDTH), BF16),
        scratch_shapes=[
            pltpu.VMEM((ATT_HEADS_PER_STEP, ATT_TILE, 1), F32),
            pltpu.VMEM((ATT_HEADS_PER_STEP, ATT_TILE, ATT_HEAD_DIM), F32),
        ],
        compiler_params=_params("arbitrary", "arbitrary", "arbitrary"),
        name="attn",
    )(proj3, proj3, proj3)


def _split3(x):
    h1 = x.astype(BF16)
    r1 = x - h1.astype(F32)
    h2 = r1.astype(BF16)
    h3 = (r1 - h2.astype(F32)).astype(BF16)
    return h1, h2, h3


def _dot3(x, m, lhs=True):
    parts = _split3(x)
    if lhs:
        outs = [jnp.dot(p, m, preferred_element_type=F32) for p in parts]
    else:
        outs = [jnp.dot(m, p, preferred_element_type=F32) for p in parts]
    return outs[0] + outs[1] + outs[2]


SSD_CHUNKS_PER_STEP = 2


def _ssd_chunk(first, z, cur, prev, dt_raw, cw_ref, cb_ref, dtb_ref, alog_ref, dskip_ref, gain_ref,
               expand_ref, state_ref):
    L = SSD_CHUNK
    out_t = lax.broadcasted_iota(jnp.int32, ((CONV_WIDTH - 1) * L, 2 * L), 0)
    src_t = lax.broadcasted_iota(jnp.int32, ((CONV_WIDTH - 1) * L, 2 * L), 1)
    shift = out_t // L + 1
    hit = src_t == L + (out_t - (shift - 1) * L) - shift
    shifts = jnp.logical_and(hit, jnp.logical_or(src_t >= L, jnp.logical_not(first))).astype(BF16)
    pieces = []
    col = 0
    for cur_blk, prev_blk in zip(cur, prev):
        width = cur_blk.shape[1]
        both = jnp.concatenate([prev_blk, cur_blk], axis=0)
        back = jnp.dot(shifts, both, preferred_element_type=F32)
        acc = cb_ref[:, col:col + width] + cur_blk.astype(F32) * cw_ref[CONV_WIDTH - 1:CONV_WIDTH, col:col + width]
        for d in range(CONV_WIDTH - 1):
            tap = CONV_WIDTH - 2 - d
            acc = acc + back[d * L:(d + 1) * L] * cw_ref[tap:tap + 1, col:col + width]
        pieces.append(_silu(acc))
        col += width
    xs = jnp.concatenate(pieces[0:2], axis=1)
    b_in = pieces[2][:, 0:SSM_GROUPS * SSM_STATE]
    c_in = pieces[2][:, SSM_GROUPS * SSM_STATE:2 * SSM_GROUPS * SSM_STATE]

    lane = lax.broadcasted_iota(jnp.int32, (1, LANES), 1)
    a2 = jnp.where(lane < SSM_HEADS, -LOG2_E * jnp.exp(alog_ref[...]), 0.0)
    dt_pre = dt_raw + dtb_ref[...]
    dt = jnp.maximum(dt_pre, 0.0) + jnp.log(1.0 + jnp.exp(-jnp.abs(dt_pre)))
    da = dt * a2
    rows = lax.broadcasted_iota(jnp.int32, (L, L), 0)
    cols = lax.broadcasted_iota(jnp.int32, (L, L), 1)
    causal = rows >= cols
    not_causal = jnp.where(causal, 0.0, -jnp.inf)
    a_cum = _dot3(da, causal.astype(BF16), lhs=False)
    src_side = (a_cum - jnp.log2(dt)).T
    a_last = a_cum[L - 1:L, :]
    expand = expand_ref[...]
    decay_in = jnp.dot(jnp.exp2(a_cum).astype(BF16), expand, preferred_element_type=F32)
    w_end = jnp.dot((jnp.exp2(a_last - a_cum) * dt).astype(BF16), expand, preferred_element_type=F32)
    chunk_decay = _dot3(jnp.broadcast_to(jnp.exp2(a_last), (SUBLANES, LANES)), expand)[0:1, :]

    xw = (xs * w_end).astype(BF16)
    lane_in_pair = lax.broadcasted_iota(jnp.int32, (L, LANES), 1)
    y_parts = []
    for g in range(SSM_GROUPS):
        bg = b_in[:, g * SSM_STATE:(g + 1) * SSM_STATE]
        cg = c_in[:, g * SSM_STATE:(g + 1) * SSM_STATE].astype(BF16)
        cb = lax.dot_general(cg, bg.astype(BF16), (((1,), (1,)), ((), ())),
                             preferred_element_type=F32)
        gcols = slice(g * GROUP_WIDTH, (g + 1) * GROUP_WIDTH)
        st = state_ref[g]
        y_off = jnp.dot(cg, st.astype(BF16), preferred_element_type=F32) * decay_in[:, gcols]
        st_new = jnp.dot(bg.T.astype(BF16), xw[:, gcols], preferred_element_type=F32)
        state_ref[g] = st * chunk_decay[:, gcols] + st_new
        heads_per_group = SSM_HEADS // SSM_GROUPS
        for pair in range(heads_per_group // 2):
            h0 = g * heads_per_group + 2 * pair
            slab = xs[:, h0 * SSM_HEAD_DIM:(h0 + 2) * SSM_HEAD_DIM]
            mats = []
            for hh in (h0, h0 + 1):
                seg = (a_cum[:, hh:hh + 1] - src_side[hh:hh + 1, :]) + not_causal
                mats.append((cb * jnp.exp2(seg)).astype(BF16))
            lhs = jnp.concatenate(mats, axis=1)
            rhs = jnp.concatenate(
                [jnp.where(lane_in_pair < SSM_HEAD_DIM, slab, 0.0),
                 jnp.where(lane_in_pair >= SSM_HEAD_DIM, slab, 0.0)], axis=0).astype(BF16)
            y_diag = jnp.dot(lhs, rhs, preferred_element_type=F32)
            lo = (h0 - g * heads_per_group) * SSM_HEAD_DIM
            y_parts.append(y_diag + y_off[:, lo:lo + LANES])
    y = jnp.concatenate(y_parts, axis=1) + xs * dskip_ref[...]
    y = y * _silu(z.astype(F32))
    outs = []
    for g in range(SSM_GROUPS):
        yg = y[:, g * GROUP_WIDTH:(g + 1) * GROUP_WIDTH]
        ms = jnp.mean(yg * yg, axis=-1, keepdims=True)
        outs.append(yg * lax.rsqrt(ms + EPS))
    return (jnp.concatenate(outs, axis=1) * gain_ref[...]).astype(BF16)


def _ssd_kernel(z_ref, x0_ref, x1_ref, bc_ref, p0_ref, p1_ref, pbc_ref, dt_ref, cw_ref, cb_ref, dtb_ref,
                alog_ref, dskip_ref, gain_ref, expand_ref, o_ref, state_ref):
    c = pl.program_id(1)
    L = SSD_CHUNK

    @pl.when(c == 0)
    def _():
        state_ref[...] = jnp.zeros_like(state_ref)

    cur_refs = (x0_ref, x1_ref, bc_ref)
    prev = tuple(r[...] for r in (p0_ref, p1_ref, pbc_ref))
    first = c == 0
    for k in range(SSD_CHUNKS_PER_STEP):
        rows = slice(k * L, (k + 1) * L)
        cur = tuple(r[rows, :] for r in cur_refs)
        o_ref[rows, :] = _ssd_chunk(first, z_ref[rows, :], cur, prev, dt_ref[rows, :], cw_ref, cb_ref, dtb_ref,
                                    alog_ref, dskip_ref, gain_ref, expand_ref, state_ref)
        prev = cur
        first = jnp.bool_(False)


def _ssd(proj3, dt3, conv_w, conv_b, dt_bias, a_log, d_skip_x, gain, expand):
    b, s, _ = proj3.shape
    L = SSD_CHUNK
    per_step = SSD_CHUNKS_PER_STEP if s % (SSD_CHUNKS_PER_STEP * L) == 0 else 1
    assert per_step == SSD_CHUNKS_PER_STEP
    rows = per_step * L
    half = SSM_WIDTH // 2
    grid = (b, s // rows)
    const = lambda shape: pl.BlockSpec(shape, lambda bi, c: (0, 0))
    xbc = lambda k: pl.BlockSpec((None, rows, half), lambda bi, c: (bi, c, XBC_OFF // half + k))
    xbc_prev = lambda k: pl.BlockSpec(
        (None, L, half), lambda bi, c: (bi, jnp.maximum(c * per_step - 1, 0), XBC_OFF // half + k))
    return pl.pallas_call(
        _ssd_kernel,
        grid=grid,
        in_specs=[
            pl.BlockSpec((None, rows, SSM_WIDTH), lambda bi, c: (bi, c, Z_OFF // SSM_WIDTH)),
            xbc(0), xbc(1), xbc(2),
            xbc_prev(0), xbc_prev(1), xbc_prev(2),
            pl.BlockSpec((None, rows, LANES), lambda bi, c: (bi, c, 0)),
            const((CONV_WIDTH, CONV_DIM)),
            const((1, CONV_DIM)),
            const((1, LANES)),
            const((1, LANES)),
            const((1, SSM_WIDTH)),
            const((1, SSM_WIDTH)),
            const((LANES, SSM_WIDTH)),
        ],
        out_specs=pl.BlockSpec((None, rows, SSM_WIDTH), lambda bi, c: (bi, c, 0)),
        out_shape=jax.ShapeDtypeStruct((b, s, SSM_WIDTH), BF16),
        scratch_shapes=[pltpu.VMEM((SSM_GROUPS, SSM_STATE, GROUP_WIDTH), F32)],
        compiler_params=_params("arbitrary", "arbitrary"),
        name="ssd",
    )(proj3, proj3, proj3, proj3, proj3, proj3, proj3, dt3, conv_w, conv_b, dt_bias, a_log, d_skip_x,
      gain, expand)


def _outproj_kernel(x_ref, oa_ref, os_ref, g_ref, w_ref, o_ref):
    a = oa_ref[...].astype(F32)
    ms = jnp.mean(a * a, axis=-1, keepdims=True)
    an = (a * lax.rsqrt(ms + EPS) * g_ref[...]).astype(BF16)
    y = jnp.dot(an, w_ref[0:ATT_WIDTH, :], preferred_element_type=F32)
    y = y + jnp.dot(os_ref[...], w_ref[ATT_WIDTH:ATT_WIDTH + SSM_WIDTH, :], preferred_element_type=F32)
    o_ref[...] = x_ref[...] + y


def _out_proj(x2, o_att, o_ssm, gain, w_out, layer, *, tm):
    t = x2.shape[0]
    return pl.pallas_call(
        _outproj_kernel,
        grid=(t // tm,),
        in_specs=[
            pl.BlockSpec((tm, D_MODEL), lambda i: (i, 0)),
            pl.BlockSpec((tm, ATT_WIDTH), lambda i: (i, 0)),
            pl.BlockSpec((tm, SSM_WIDTH), lambda i: (i, 0)),
            pl.BlockSpec((1, ATT_WIDTH), lambda i: (0, 0)),
            pl.BlockSpec((None, ATT_WIDTH + SSM_WIDTH, D_MODEL), lambda i: (layer, 0, 0)),
        ],
        out_specs=pl.BlockSpec((tm, D_MODEL), lambda i: (i, 0)),
        out_shape=jax.ShapeDtypeStruct((t, D_MODEL), F32),
        compiler_params=_params("arbitrary"),
        name="out_proj",
    )(x2, o_att, o_ssm, gain, w_out)


FFN_FIRST_STEP_ROW_GROUPS = 2


def _ffn_kernel(x_ref, g_ref, wg_ref, wu_ref, wd_ref, o_ref, h_ref):
    f = pl.program_id(1)

    def swiglu(h):
        gate = jnp.dot(h, wg_ref[...], preferred_element_type=F32)
        up = jnp.dot(h, wu_ref[...], preferred_element_type=F32)
        act = (_silu(gate) * up).astype(BF16)
        return jnp.dot(act, wd_ref[...], preferred_element_type=F32)

    @pl.when(f == 0)
    def _():
        rows = x_ref.shape[0] // FFN_FIRST_STEP_ROW_GROUPS
        for r in range(FFN_FIRST_STEP_ROW_GROUPS):
            sl = slice(r * rows, (r + 1) * rows)
            x = x_ref[sl, :]
            ms = jnp.mean(x * x, axis=-1, keepdims=True)
            h = (x * lax.rsqrt(ms + EPS) * g_ref[...]).astype(BF16)
            h_ref[sl, :] = h
            o_ref[sl, :] = x + swiglu(h)

    @pl.when(f > 0)
    def _():
        o_ref[...] += swiglu(h_ref[...])


def _ffn(x2, g, w_gate, w_up, w_down, layer, *, tm, tf):
    t = x2.shape[0]
    return pl.pallas_call(
        _ffn_kernel,
        grid=(t // tm, D_FF // tf),
        in_specs=[
            pl.BlockSpec((tm, D_MODEL), lambda i, f: (i, 0)),
            pl.BlockSpec((1, D_MODEL), lambda i, f: (0, 0)),
            pl.BlockSpec((None, D_MODEL, tf), lambda i, f: (layer, 0, f)),
            pl.BlockSpec((None, D_MODEL, tf), lambda i, f: (layer, 0, f)),
            pl.BlockSpec((None, tf, D_MODEL), lambda i, f: (layer, f, 0)),
        ],
        out_specs=pl.BlockSpec((tm, D_MODEL), lambda i, f: (i, 0)),
        out_shape=jax.ShapeDtypeStruct((t, D_MODEL), F32),
        scratch_shapes=[pltpu.VMEM((tm, D_MODEL), BF16)],
        compiler_params=_params("arbitrary", "arbitrary"),
        name="ffn",
    )(x2, g, w_gate, w_up, w_down)


def _tile(pref, n):
    return pref if n % pref == 0 else n


def _layer(x, layer, norm_mix, q_gain, k_gain, conv_w, conv_b, dt_bias, a_log, d_skip,
           attn_out_gain, ssm_out_gain, norm_ffn, w_main, w_dt, w_out, w_gate, w_up, w_down, expand):
    b, s, _ = x.shape
    t = b * s
    x2 = x.reshape(t, D_MODEL)
    row = lambda v: v.reshape(1, -1).astype(F32)
    pad_lanes = lambda v: jnp.pad(v.astype(F32), (0, LANES - v.shape[0])).reshape(1, LANES)

    proj, dt_raw = _in_proj(x2, row(norm_mix), w_main, w_dt, layer,
                            row(q_gain) * (LOG2_E * ATT_HEAD_DIM ** -0.5), row(k_gain),
                            tm=_tile(512, t), tn=MAIN_DIM // 2)
    proj3 = proj.reshape(b, s, MAIN_DIM)

    o_att = _attention(proj3)
    o_ssm = _ssd(proj3, dt_raw.reshape(b, s, LANES), conv_w.astype(F32), row(conv_b),
                 pad_lanes(dt_bias), pad_lanes(a_log),
                 row(jnp.repeat(d_skip, SSM_HEAD_DIM)), row(ssm_out_gain), expand)

    x2 = _out_proj(x2, o_att.reshape(t, ATT_WIDTH), o_ssm.reshape(t, SSM_WIDTH),
                   row(attn_out_gain), w_out, layer, tm=_tile(512, t))
    x2 = _ffn(x2, row(norm_ffn), w_gate, w_up, w_down, layer, tm=_tile(1024, t), tf=512)
    return x2.reshape(b, s, D_MODEL)


def _bf16_weights(w_in, w_out, w_gate, w_up, w_down):
    w_main = w_in.astype(BF16)
    w_dt = jnp.pad(w_in[:, :, MAIN_DIM:], ((0, 0), (0, 0), (0, LANES - SSM_HEADS))).astype(BF16)
    return w_main, w_dt, w_out.astype(BF16), w_gate.astype(BF16), w_up.astype(BF16), w_down.astype(BF16)


def _head_expand():
    head_of_lane = jnp.arange(SSM_WIDTH, dtype=jnp.int32) // SSM_HEAD_DIM
    return (jnp.arange(LANES, dtype=jnp.int32)[:, None] == head_of_lane[None, :]).astype(BF16)


def kernel(x, norm_mix, w_in, q_gain, k_gain, conv_w, conv_b, dt_bias, a_log, d_skip, attn_out_gain, ssm_out_gain, w_out, norm_ffn, w_gate, w_up, w_down):
    weights = _bf16_weights(w_in, w_out, w_gate, w_up, w_down)
    expand = _head_expand()
    for i in range(norm_mix.shape[0]):
        x = _layer(x, i, norm_mix[i], q_gain[i], k_gain[i], conv_w[i], conv_b[i], dt_bias[i], a_log[i],
                   d_skip[i], attn_out_gain[i], ssm_out_gain[i], norm_ffn[i], *weights, expand)
    return x
```

```python
import functools

import jax
import jax.numpy as jnp
from jax import lax
from jax.experimental import pallas as pl
from jax.experimental.pallas import tpu as pltpu

F32 = jnp.float32
BF16 = jnp.bfloat16

D_MODEL = 2048
ATT_WIDTH = 1024
SSM_WIDTH = 1024
ATT_HEAD_DIM = 128
ATT_HEADS = ATT_WIDTH // ATT_HEAD_DIM
SSM_HEAD_DIM = 64
SSM_HEADS = SSM_WIDTH // SSM_HEAD_DIM
SSM_GROUPS = 2
SSM_STATE = 128
GROUP_WIDTH = SSM_WIDTH // SSM_GROUPS
CONV_WIDTH = 4
CONV_DIM = SSM_WIDTH + 2 * SSM_GROUPS * SSM_STATE
SSD_CHUNK = 128
D_FF = 5632
EPS = 1e-6
LOG2_E = 1.4426950408889634

LANES = 128
SUBLANES = 8
VMEM_LIMIT = 56 * 1024 * 1024

MAIN_DIM = 3 * ATT_WIDTH + SSM_WIDTH + CONV_DIM
Q_OFF, K_OFF, V_OFF = 0, ATT_WIDTH, 2 * ATT_WIDTH
Z_OFF = 3 * ATT_WIDTH
XBC_OFF = Z_OFF + SSM_WIDTH


def _params(*sem):
    return pltpu.CompilerParams(dimension_semantics=sem, vmem_limit_bytes=VMEM_LIMIT)


def _silu(x):
    h = 0.5 * x
    return h + h * jnp.tanh(h)


INPROJ_SUB = 512


def _inproj_kernel(x_ref, g_ref, w_ref, wdt_ref, qg_ref, kg_ref, o_ref, dt_ref, h_ref, *, tn):
    j = pl.program_id(1)
    assert tn >= K_OFF + ATT_WIDTH

    def columns(first_tile):
        h = h_ref[...]
        for lo in range(0, tn, INPROJ_SUB):
            hi = min(lo + INPROJ_SUB, tn)
            acc = jnp.dot(h, w_ref[:, lo:hi], preferred_element_type=F32)
            for c0 in range(lo, hi, ATT_HEAD_DIM):
                seg = acc[:, c0 - lo:c0 - lo + ATT_HEAD_DIM]
                if first_tile and c0 < K_OFF + ATT_WIDTH:
                    gain = qg_ref[...] if c0 < K_OFF else kg_ref[...]
                    ms = jnp.mean(seg * seg, axis=-1, keepdims=True)
                    seg = seg * lax.rsqrt(ms + EPS) * gain
                o_ref[:, c0:c0 + ATT_HEAD_DIM] = seg.astype(BF16)

    @pl.when(j == 0)
    def _():
        x = x_ref[...]
        ms = jnp.mean(x * x, axis=-1, keepdims=True)
        h = (x * lax.rsqrt(ms + EPS) * g_ref[...]).astype(BF16)
        h_ref[...] = h
        dt_ref[...] = jnp.dot(h, wdt_ref[...], preferred_element_type=F32)
        columns(True)

    @pl.when(j > 0)
    def _():
        columns(False)


def _in_proj(x2, g, w_main, w_dt, layer, qg, kg, *, tm, tn):
    t = x2.shape[0]
    grid = (t // tm, MAIN_DIM // tn)
    return pl.pallas_call(
        functools.partial(_inproj_kernel, tn=tn),
        grid=grid,
        in_specs=[
            pl.BlockSpec((tm, D_MODEL), lambda i, j: (i, 0)),
            pl.BlockSpec((1, D_MODEL), lambda i, j: (0, 0)),
            pl.BlockSpec((None, D_MODEL, tn), lambda i, j: (layer, 0, j)),
            pl.BlockSpec((None, D_MODEL, LANES), lambda i, j: (layer, 0, 0)),
            pl.BlockSpec((1, ATT_HEAD_DIM), lambda i, j: (0, 0)),
            pl.BlockSpec((1, ATT_HEAD_DIM), lambda i, j: (0, 0)),
        ],
        out_specs=[
            pl.BlockSpec((tm, tn), lambda i, j: (i, j)),
            pl.BlockSpec((tm, LANES), lambda i, j: (i, 0)),
        ],
        out_shape=[
            jax.ShapeDtypeStruct((t, MAIN_DIM), BF16),
            jax.ShapeDtypeStruct((t, LANES), F32),
        ],
        scratch_shapes=[pltpu.VMEM((tm, D_MODEL), BF16)],
        compiler_params=_params("arbitrary", "arbitrary"),
        name="in_proj",
    )(x2, g, w_main, w_dt, qg, kg)


ATT_TILE = 256
ATT_HEADS_PER_STEP = 8
LOG2_WEIGHT_FLOOR = -152.0


def _attn_kernel(q_ref, k_ref, v_ref, o_ref, r_ref, acc_ref):
    t = ATT_TILE
    heads = ATT_HEADS_PER_STEP
    i = pl.program_id(2)
    rows = lax.broadcasted_iota(jnp.int32, (t, t), 0)
    cols = lax.broadcasted_iota(jnp.int32, (t, t), 1)
    later_sum = (rows > cols).astype(BF16)
    causal = cols < rows
    sign_bit = jnp.uint32(0x80000000)
    lanes = [slice(hh * ATT_HEAD_DIM, (hh + 1) * ATT_HEAD_DIM) for hh in range(heads)]

    def step(jblk, first):
        ks = pl.multiple_of(jblk * t, t)
        log_betas, log_remains, ws = [None] * heads, [None] * heads, [None] * heads
        alive = None
        for s in range(heads + 2):
            if s < heads:
                z = lax.dot_general(q_ref[:, lanes[s]], k_ref[pl.ds(ks, t), lanes[s]],
                                    (((1,), (1,)), ((), ())), preferred_element_type=F32)
                neg_abs = lax.bitcast_convert_type(lax.bitcast_convert_type(z, jnp.uint32) | sign_bit, F32)
                log_beta = jnp.minimum(z, 0.0) - jnp.log2(1.0 + jnp.exp2(neg_abs))
                log_remain = log_beta - z
                if first:
                    log_remain = jnp.where(causal, log_remain, 0.0)
                log_betas[s], log_remains[s] = log_beta, log_remain
            if 0 <= s - 1 < heads:
                hh = s - 1
                within = jnp.dot(log_remains[hh].astype(BF16), later_sum, preferred_element_type=F32)
                w = jnp.exp2(log_betas[hh] + within)
                if first:
                    w = jnp.where(causal, w, 0.0)
                ws[hh] = w.astype(BF16)
            if 0 <= s - 2 < heads:
                hh = s - 2
                pv = jnp.dot(ws[hh], v_ref[pl.ds(ks, t), lanes[hh]], preferred_element_type=F32)
                block_sum = jnp.sum(log_remains[hh], axis=1, keepdims=True)
                if first:
                    acc_ref[hh] = pv
                    r_new = block_sum
                else:
                    r_old = r_ref[hh]
                    acc_ref[hh] += jnp.exp2(r_old) * pv
                    r_new = r_old + block_sum
                r_ref[hh] = r_new
                alive = r_new if alive is None else jnp.maximum(alive, r_new)
        return (jnp.max(alive) > LOG2_WEIGHT_FLOOR).astype(jnp.int32)

    go = step(i, True)

    def cond(c):
        n, go = c
        return jnp.logical_and(n < i, go > 0)

    def body(c):
        n, _ = c
        return n + 1, step(i - 1 - n, False)

    lax.while_loop(cond, body, (jnp.int32(0), go))
    for hh in range(heads):
        o_ref[:, lanes[hh]] = acc_ref[hh].astype(o_ref.dtype)


def _attention(proj3):
    b, s, _ = proj3.shape
    width = ATT_HEADS_PER_STEP * ATT_HEAD_DIM
    grid = (b, ATT_HEADS // ATT_HEADS_PER_STEP, s // ATT_TILE)
    qb, kb, vb = Q_OFF // width, K_OFF // width, V_OFF // width
    return pl.pallas_call(
        _attn_kernel,
        grid=grid,
        in_specs=[
            pl.BlockSpec((None, ATT_TILE, width), lambda bi, h, i: (bi, i, qb + h)),
            pl.BlockSpec((None, s, width), lambda bi, h, i: (bi, 0, kb + h)),
            pl.BlockSpec((None, s, width), lambda bi, h, i: (bi, 0, vb + h)),
        ],
        out_specs=pl.BlockSpec((None, ATT_TILE, width), lambda bi, h, i: (bi, i, h)),
        out_shape=jax.ShapeDtypeStruct((b, s, ATT_WIDTH), BF16),
        scratch_shapes=[
            pltpu.VMEM((ATT_HEADS_PER_STEP, ATT_TILE, 1), F32),
            pltpu.VMEM((ATT_HEADS_PER_STEP, ATT_TILE, ATT_HEAD_DIM), F32),
        ],
        compiler_params=_params("arbitrary", "arbitrary", "arbitrary"),
        name="attn",
    )(proj3, proj3, proj3)


def _split3(x):
    h1 = x.astype(BF16)
    r1 = x - h1.astype(F32)
    h2 = r1.astype(BF16)
    h3 = (r1 - h2.astype(F32)).astype(BF16)
    return h1, h2, h3


def _dot3(x, m, lhs=True):
    parts = _split3(x)
    if lhs:
        outs = [jnp.dot(p, m, preferred_element_type=F32) for p in parts]
    else:
        outs = [jnp.dot(m, p, preferred_element_type=F32) for p in parts]
    return outs[0] + outs[1] + outs[2]


SSD_CHUNKS_PER_STEP = 4


def _ssd_chunk(first, z, cur, prev, dt_raw, cw_ref, cb_ref, dtb_ref, alog_ref, dskip_ref, gain_ref,
               expand_ref, state_ref):
    L = SSD_CHUNK
    out_t = lax.broadcasted_iota(jnp.int32, ((CONV_WIDTH - 1) * L, 2 * L), 0)
    src_t = lax.broadcasted_iota(jnp.int32, ((CONV_WIDTH - 1) * L, 2 * L), 1)
    shift = out_t // L + 1
    hit = src_t == L + (out_t - (shift - 1) * L) - shift
    shifts = jnp.logical_and(hit, jnp.logical_or(src_t >= L, jnp.logical_not(first))).astype(BF16)
    pieces = []
    col = 0
    for cur_blk, prev_blk in zip(cur, prev):
        width = cur_blk.shape[1]
        both = jnp.concatenate([prev_blk, cur_blk], axis=0)
        back = jnp.dot(shifts, both, preferred_element_type=F32)
        acc = cb_ref[:, col:col + width] + cur_blk.astype(F32) * cw_ref[CONV_WIDTH - 1:CONV_WIDTH, col:col + width]
        for d in range(CONV_WIDTH - 1):
            tap = CONV_WIDTH - 2 - d
            acc = acc + back[d * L:(d + 1) * L] * cw_ref[tap:tap + 1, col:col + width]
        pieces.append(_silu(acc))
        col += width
    xs = jnp.concatenate(pieces[0:2], axis=1)
    b_in = pieces[2][:, 0:SSM_GROUPS * SSM_STATE]
    c_in = pieces[2][:, SSM_GROUPS * SSM_STATE:2 * SSM_GROUPS * SSM_STATE]

    lane = lax.broadcasted_iota(jnp.int32, (1, LANES), 1)
    a2 = jnp.where(lane < SSM_HEADS, -LOG2_E * jnp.exp(alog_ref[...]), 0.0)
    dt_pre = dt_raw + dtb_ref[...]
    dt = jnp.maximum(dt_pre, 0.0) + jnp.log(1.0 + jnp.exp(-jnp.abs(dt_pre)))
    da = dt * a2
    rows = lax.broadcasted_iota(jnp.int32, (L, L), 0)
    cols = lax.broadcasted_iota(jnp.int32, (L, L), 1)
    causal = rows >= cols
    not_causal = jnp.where(causal, 0.0, -jnp.inf)
    a_cum = _dot3(da, causal.astype(BF16), lhs=False)
    src_side = (a_cum - jnp.log2(dt)).T
    a_last = a_cum[L - 1:L, :]
    expand = expand_ref[...]
    decay_in = jnp.dot(jnp.exp2(a_cum).astype(BF16), expand, preferred_element_type=F32)
    w_end = jnp.dot((jnp.exp2(a_last - a_cum) * dt).astype(BF16), expand, preferred_element_type=F32)
    chunk_decay = _dot3(jnp.broadcast_to(jnp.exp2(a_last), (SUBLANES, LANES)), expand)[0:1, :]

    xw = (xs * w_end).astype(BF16)
    lane_in_pair = lax.broadcasted_iota(jnp.int32, (L, LANES), 1)
    y_parts = []
    for g in range(SSM_GROUPS):
        bg = b_in[:, g * SSM_STATE:(g + 1) * SSM_STATE]
        cg = c_in[:, g * SSM_STATE:(g + 1) * SSM_STATE].astype(BF16)
        cb = lax.dot_general(cg, bg.astype(BF16), (((1,), (1,)), ((), ())),
                             preferred_element_type=F32)
        gcols = slice(g * GROUP_WIDTH, (g + 1) * GROUP_WIDTH)
        st = state_ref[g]
        y_off = jnp.dot(cg, st.astype(BF16), preferred_element_type=F32) * decay_in[:, gcols]
        st_new = jnp.dot(bg.T.astype(BF16), xw[:, gcols], preferred_element_type=F32)
        state_ref[g] = st * chunk_decay[:, gcols] + st_new
        heads_per_group = SSM_HEADS // SSM_GROUPS
        for pair in range(heads_per_group // 2):
            h0 = g * heads_per_group + 2 * pair
            slab = xs[:, h0 * SSM_HEAD_DIM:(h0 + 2) * SSM_HEAD_DIM]
            mats = []
            for hh in (h0, h0 + 1):
                seg = (a_cum[:, hh:hh + 1] - src_side[hh:hh + 1, :]) + not_causal
                mats.append((cb * jnp.exp2(seg)).astype(BF16))
            lhs = jnp.concatenate(mats, axis=1)
            rhs = jnp.concatenate(
                [jnp.where(lane_in_pair < SSM_HEAD_DIM, slab, 0.0),
                 jnp.where(lane_in_pair >= SSM_HEAD_DIM, slab, 0.0)], axis=0).astype(BF16)
            y_diag = jnp.dot(lhs, rhs, preferred_element_type=F32)
            lo = (h0 - g * heads_per_group) * SSM_HEAD_DIM
            y_parts.append(y_diag + y_off[:, lo:lo + LANES])
    y = jnp.concatenate(y_parts, axis=1) + xs * dskip_ref[...]
    y = y * _silu(z.astype(F32))
    outs = []
    for g in range(SSM_GROUPS):
        yg = y[:, g * GROUP_WIDTH:(g + 1) * GROUP_WIDTH]
        ms = jnp.mean(yg * yg, axis=-1, keepdims=True)
        outs.append(yg * lax.rsqrt(ms + EPS))
    return (jnp.concatenate(outs, axis=1) * gain_ref[...]).astype(BF16)


def _ssd_kernel(z_ref, x0_ref, x1_ref, bc_ref, p0_ref, p1_ref, pbc_ref, dt_ref, cw_ref, cb_ref, dtb_ref,
                alog_ref, dskip_ref, gain_ref, expand_ref, o_ref, state_ref):
    c = pl.program_id(1)
    L = SSD_CHUNK

    @pl.when(c == 0)
    def _():
        state_ref[...] = jnp.zeros_like(state_ref)

    cur_refs = (x0_ref, x1_ref, bc_ref)
    prev = tuple(r[...] for r in (p0_ref, p1_ref, pbc_ref))
    first = c == 0
    for k in range(SSD_CHUNKS_PER_STEP):
        rows = slice(k * L, (k + 1) * L)
        cur = tuple(r[rows, :] for r in cur_refs)
        o_ref[rows, :] = _ssd_chunk(first, z_ref[rows, :], cur, prev, dt_ref[rows, :], cw_ref, cb_ref, dtb_ref,
                                    alog_ref, dskip_ref, gain_ref, expand_ref, state_ref)
        prev = cur
        first = jnp.bool_(False)


def _ssd(proj3, dt3, conv_w, conv_b, dt_bias, a_log, d_skip_x, gain, expand):
    b, s, _ = proj3.shape
    L = SSD_CHUNK
    per_step = SSD_CHUNKS_PER_STEP if s % (SSD_CHUNKS_PER_STEP * L) == 0 else 1
    assert per_step == SSD_CHUNKS_PER_STEP
    rows = per_step * L
    half = SSM_WIDTH // 2
    grid = (b, s // rows)
    const = lambda shape: pl.BlockSpec(shape, lambda bi, c: (0, 0))
    xbc = lambda k: pl.BlockSpec((None, rows, half), lambda bi, c: (bi, c, XBC_OFF // half + k))
    xbc_prev = lambda k: pl.BlockSpec(
        (None, L, half), lambda bi, c: (bi, jnp.maximum(c * per_step - 1, 0), XBC_OFF // half + k))
    return pl.pallas_call(
        _ssd_kernel,
        grid=grid,
        in_specs=[
            pl.BlockSpec((None, rows, SSM_WIDTH), lambda bi, c: (bi, c, Z_OFF // SSM_WIDTH)),
            xbc(0), xbc(1), xbc(2),
            xbc_prev(0), xbc_prev(1), xbc_prev(2),
            pl.BlockSpec((None, rows, LANES), lambda bi, c: (bi, c, 0)),
            const((CONV_WIDTH, CONV_DIM)),
            const((1, CONV_DIM)),
            const((1, LANES)),
            const((1, LANES)),
            const((1, SSM_WIDTH)),
            const((1, SSM_WIDTH)),
            const((LANES, SSM_WIDTH)),
        ],
        out_specs=pl.BlockSpec((None, rows, SSM_WIDTH), lambda bi, c: (bi, c, 0)),
        out_shape=jax.ShapeDtypeStruct((b, s, SSM_WIDTH), BF16),
        scratch_shapes=[pltpu.VMEM((SSM_GROUPS, SSM_STATE, GROUP_WIDTH), F32)],
        compiler_params=_params("arbitrary", "arbitrary"),
        name="ssd",
    )(proj3, proj3, proj3, proj3, proj3, proj3, proj3, dt3, conv_w, conv_b, dt_bias, a_log, d_skip_x,
      gain, expand)


OUTPROJ_ROW_GROUPS = 2


def _outproj_kernel(x_ref, oa_ref, os_ref, g_ref, w_ref, o_ref):
    rows = x_ref.shape[0] // OUTPROJ_ROW_GROUPS
    for r in range(OUTPROJ_ROW_GROUPS):
        sl = slice(r * rows, (r + 1) * rows)
        a = oa_ref[sl, :].astype(F32)
        ms = jnp.mean(a * a, axis=-1, keepdims=True)
        an = (a * lax.rsqrt(ms + EPS) * g_ref[...]).astype(BF16)
        y = jnp.dot(an, w_ref[0:ATT_WIDTH, :], preferred_element_type=F32)
        y = y + jnp.dot(os_ref[sl, :], w_ref[ATT_WIDTH:ATT_WIDTH + SSM_WIDTH, :], preferred_element_type=F32)
        o_ref[sl, :] = x_ref[sl, :] + y


def _out_proj(x2, o_att, o_ssm, gain, w_out, layer, *, tm):
    t = x2.shape[0]
    return pl.pallas_call(
        _outproj_kernel,
        grid=(t // tm,),
        in_specs=[
            pl.BlockSpec((tm, D_MODEL), lambda i: (i, 0)),
            pl.BlockSpec((tm, ATT_WIDTH), lambda i: (i, 0)),
            pl.BlockSpec((tm, SSM_WIDTH), lambda i: (i, 0)),
            pl.BlockSpec((1, ATT_WIDTH), lambda i: (0, 0)),
            pl.BlockSpec((None, ATT_WIDTH + SSM_WIDTH, D_MODEL), lambda i: (layer, 0, 0),
                         pipeline_mode=pl.Buffered(1)),
        ],
        out_specs=pl.BlockSpec((tm, D_MODEL), lambda i: (i, 0)),
        out_shape=jax.ShapeDtypeStruct((t, D_MODEL), F32),
        compiler_params=_params("arbitrary"),
        name="out_proj",
    )(x2, o_att, o_ssm, gain, w_out)


FFN_FIRST_STEP_ROW_GROUPS = 2


def _ffn_kernel(x_ref, g_ref, wg_ref, wu_ref, wd_ref, o_ref, h_ref):
    f = pl.program_id(1)

    def swiglu(h):
        gate = jnp.dot(h, wg_ref[...], preferred_element_type=F32)
        up = jnp.dot(h, wu_ref[...], preferred_element_type=F32)
        act = (_silu(gate) * up).astype(BF16)
        return jnp.dot(act, wd_ref[...], preferred_element_type=F32)

    @pl.when(f == 0)
    def _():
        rows = x_ref.shape[0] // FFN_FIRST_STEP_ROW_GROUPS
        for r in range(FFN_FIRST_STEP_ROW_GROUPS):
            sl = slice(r * rows, (r + 1) * rows)
            x = x_ref[sl, :]
            ms = jnp.mean(x * x, axis=-1, keepdims=True)
            h = (x * lax.rsqrt(ms + EPS) * g_ref[...]).astype(BF16)
            h_ref[sl, :] = h
            o_ref[sl, :] = x + swiglu(h)

    @pl.when(f > 0)
    def _():
        o_ref[...] += swiglu(h_ref[...])


def _ffn(x2, g, w_gate, w_up, w_down, layer, *, tm, tf):
    t = x2.shape[0]
    return pl.pallas_call(
        _ffn_kernel,
        grid=(t // tm, D_FF // tf),
        in_specs=[
            pl.BlockSpec((tm, D_MODEL), lambda i, f: (i, 0)),
            pl.BlockSpec((1, D_MODEL), lambda i, f: (0, 0)),
            pl.BlockSpec((None, D_MODEL, tf), lambda i, f: (layer, 0, f)),
            pl.BlockSpec((None, D_MODEL, tf), lambda i, f: (layer, 0, f)),
            pl.BlockSpec((None, tf, D_MODEL), lambda i, f: (layer, f, 0)),
        ],
        out_specs=pl.BlockSpec((tm, D_MODEL), lambda i, f: (i, 0)),
        out_shape=jax.ShapeDtypeStruct((t, D_MODEL), F32),
        scratch_shapes=[pltpu.VMEM((tm, D_MODEL), BF16)],
        compiler_params=_params("arbitrary", "arbitrary"),
        name="ffn",
    )(x2, g, w_gate, w_up, w_down)


def _tile(pref, n):
    return pref if n % pref == 0 else n


def _layer(x, layer, norm_mix, q_gain, k_gain, conv_w, conv_b, dt_bias, a_log, d_skip,
           attn_out_gain, ssm_out_gain, norm_ffn, w_main, w_dt, w_out, w_gate, w_up, w_down, expand):
    b, s, _ = x.shape
    t = b * s
    x2 = x.reshape(t, D_MODEL)
    row = lambda v: v.reshape(1, -1).astype(F32)
    pad_lanes = lambda v: jnp.pad(v.astype(F32), (0, LANES - v.shape[0])).reshape(1, LANES)

    proj, dt_raw = _in_proj(x2, row(norm_mix), w_main, w_dt, layer,
                            row(q_gain) * (LOG2_E * ATT_HEAD_DIM ** -0.5), row(k_gain),
                            tm=_tile(512, t), tn=MAIN_DIM // 2)
    proj3 = proj.reshape(b, s, MAIN_DIM)

    o_att = _attention(proj3)
    o_ssm = _ssd(proj3, dt_raw.reshape(b, s, LANES), conv_w.astype(F32), row(conv_b),
                 pad_lanes(dt_bias), pad_lanes(a_log),
                 row(jnp.repeat(d_skip, SSM_HEAD_DIM)), row(ssm_out_gain), expand)

    x2 = _out_proj(x2, o_att.reshape(t, ATT_WIDTH), o_ssm.reshape(t, SSM_WIDTH),
                   row(attn_out_gain), w_out, layer, tm=_tile(1024, t))
    x2 = _ffn(x2, row(norm_ffn), w_gate, w_up, w_down, layer, tm=_tile(1024, t), tf=512)
    return x2.reshape(b, s, D_MODEL)


def _bf16_weights(w_in, w_out, w_gate, w_up, w_down):
    w_main = w_in.astype(BF16)
    w_dt = jnp.pad(w_in[:, :, MAIN_DIM:], ((0, 0), (0, 0), (0, LANES - SSM_HEADS))).astype(BF16)
    return w_main, w_dt, w_out.astype(BF16), w_gate.astype(BF16), w_up.astype(BF16), w_down.astype(BF16)


def _head_expand():
    head_of_lane = jnp.arange(SSM_WIDTH, dtype=jnp.int32) // SSM_HEAD_DIM
    return (jnp.arange(LANES, dtype=jnp.int32)[:, None] == head_of_lane[None, :]).astype(BF16)


def kernel(x, norm_mix, w_in, q_gain, k_gain, conv_w, conv_b, dt_bias, a_log, d_skip, attn_out_gain, ssm_out_gain, w_out, norm_ffn, w_gate, w_up, w_down):
    weights = _bf16_weights(w_in, w_out, w_gate, w_up, w_down)
    expand = _head_expand()
    for i in range(norm_mix.shape[0]):
        x = _layer(x, i, norm_mix[i], q_gain[i], k_gain[i], conv_w[i], conv_b[i], dt_bias[i], a_log[i],
                   d_skip[i], attn_out_gain[i], ssm_out_gain[i], norm_ffn[i], *weights, expand)
    return x
```

```python
import functools

import jax
import jax.numpy as jnp
from jax import lax
from jax.experimental import pallas as pl
from jax.experimental.pallas import tpu as pltpu

F32 = jnp.float32
BF16 = jnp.bfloat16

D_MODEL = 2048
ATT_WIDTH = 1024
SSM_WIDTH = 1024
ATT_HEAD_DIM = 128
ATT_HEADS = ATT_WIDTH // ATT_HEAD_DIM
SSM_HEAD_DIM = 64
SSM_HEADS = SSM_WIDTH // SSM_HEAD_DIM
SSM_GROUPS = 2
SSM_STATE = 128
GROUP_WIDTH = SSM_WIDTH // SSM_GROUPS
CONV_WIDTH = 4
CONV_DIM = SSM_WIDTH + 2 * SSM_GROUPS * SSM_STATE
SSD_CHUNK = 128
D_FF = 5632
EPS = 1e-6
LOG2_E = 1.4426950408889634

LANES = 128
SUBLANES = 8
BF16_SUBLANES = 16
VMEM_LIMIT = 56 * 1024 * 1024

MAIN_DIM = 3 * ATT_WIDTH + SSM_WIDTH + CONV_DIM
Q_OFF, K_OFF, V_OFF = 0, ATT_WIDTH, 2 * ATT_WIDTH
Z_OFF = 3 * ATT_WIDTH
XBC_OFF = Z_OFF + SSM_WIDTH


def _params(*sem):
    return pltpu.CompilerParams(dimension_semantics=sem, vmem_limit_bytes=VMEM_LIMIT)


def _silu(x):
    h = 0.5 * x
    return h + h * jnp.tanh(h)


INPROJ_SUB = 512


def _inproj_kernel(x_ref, g_ref, w_ref, wdt_ref, qg_ref, kg_ref, o_ref, dt_ref, h_ref, *, tn):
    j = pl.program_id(1)
    assert tn >= K_OFF + ATT_WIDTH

    def columns(first_tile):
        h = h_ref[...]
        for lo in range(0, tn, INPROJ_SUB):
            hi = min(lo + INPROJ_SUB, tn)
            acc = jnp.dot(h, w_ref[:, lo:hi], preferred_element_type=F32)
            for c0 in range(lo, hi, ATT_HEAD_DIM):
                seg = acc[:, c0 - lo:c0 - lo + ATT_HEAD_DIM]
                if first_tile and c0 < K_OFF + ATT_WIDTH:
                    gain = qg_ref[...] if c0 < K_OFF else kg_ref[...]
                    ms = jnp.mean(seg * seg, axis=-1, keepdims=True)
                    seg = seg * lax.rsqrt(ms + EPS) * gain
                o_ref[:, c0:c0 + ATT_HEAD_DIM] = seg.astype(BF16)

    @pl.when(j == 0)
    def _():
        x = x_ref[...]
        ms = jnp.mean(x * x, axis=-1, keepdims=True)
        h = (x * lax.rsqrt(ms + EPS) * g_ref[...]).astype(BF16)
        h_ref[...] = h
        dt_ref[...] = jnp.dot(h, wdt_ref[...], preferred_element_type=F32)
        columns(True)

    @pl.when(j > 0)
    def _():
        columns(False)


def _in_proj(x2, g, w_main, w_dt, layer, qg, kg, *, tm, tn):
    t = x2.shape[0]
    grid = (t // tm, MAIN_DIM // tn)
    return pl.pallas_call(
        functools.partial(_inproj_kernel, tn=tn),
        grid=grid,
        in_specs=[
            pl.BlockSpec((tm, D_MODEL), lambda i, j: (i, 0)),
            pl.BlockSpec((1, D_MODEL), lambda i, j: (0, 0)),
            pl.BlockSpec((None, D_MODEL, tn), lambda i, j: (layer, 0, j)),
            pl.BlockSpec((None, D_MODEL, LANES), lambda i, j: (layer, 0, 0)),
            pl.BlockSpec((1, ATT_HEAD_DIM), lambda i, j: (0, 0)),
            pl.BlockSpec((1, ATT_HEAD_DIM), lambda i, j: (0, 0)),
        ],
        out_specs=[
            pl.BlockSpec((tm, tn), lambda i, j: (i, j)),
            pl.BlockSpec((tm, LANES), lambda i, j: (i, 0)),
        ],
        out_shape=[
            jax.ShapeDtypeStruct((t, MAIN_DIM), BF16),
            jax.ShapeDtypeStruct((t, LANES), F32),
        ],
        scratch_shapes=[pltpu.VMEM((tm, D_MODEL), BF16)],
        compiler_params=_params("arbitrary", "arbitrary"),
        name="in_proj",
    )(x2, g, w_main, w_dt, qg, kg)


ATT_TILE = 256
ATT_HEADS_PER_STEP = 8
LOG2_WEIGHT_FLOOR = -152.0


def _attn_kernel(q_ref, k_ref, v_ref, o_ref, r_ref, acc_ref):
    t = ATT_TILE
    heads = ATT_HEADS_PER_STEP
    i = pl.program_id(2)
    rows = lax.broadcasted_iota(jnp.int32, (t, t), 0)
    cols = lax.broadcasted_iota(jnp.int32, (t, t), 1)
    later_sum = (rows > cols).astype(BF16)
    causal = cols < rows
    sign_bit = jnp.uint32(0x80000000)
    lanes = [slice(hh * ATT_HEAD_DIM, (hh + 1) * ATT_HEAD_DIM) for hh in range(heads)]

    def step(jblk, first):
        ks = pl.multiple_of(jblk * t, t)
        log_betas, log_remains, ws = [None] * heads, [None] * heads, [None] * heads
        alive = None
        for s in range(heads + 2):
            if s < heads:
                z = lax.dot_general(q_ref[:, lanes[s]], k_ref[pl.ds(ks, t), lanes[s]],
                                    (((1,), (1,)), ((), ())), preferred_element_type=F32)
                neg_abs = lax.bitcast_convert_type(lax.bitcast_convert_type(z, jnp.uint32) | sign_bit, F32)
                log_beta = jnp.minimum(z, 0.0) - jnp.log2(1.0 + jnp.exp2(neg_abs))
                log_remain = log_beta - z
                if first:
                    log_remain = jnp.where(causal, log_remain, 0.0)
                log_betas[s], log_remains[s] = log_beta, log_remain
            if 0 <= s - 1 < heads:
                hh = s - 1
                within = jnp.dot(log_remains[hh].astype(BF16), later_sum, preferred_element_type=F32)
                w = jnp.exp2(log_betas[hh] + within)
                if first:
                    w = jnp.where(causal, w, 0.0)
                ws[hh] = w.astype(BF16)
            if 0 <= s - 2 < heads:
                hh = s - 2
                pv = jnp.dot(ws[hh], v_ref[pl.ds(ks, t), lanes[hh]], preferred_element_type=F32)
                block_sum = jnp.sum(log_remains[hh], axis=1, keepdims=True)
                if first:
                    acc_ref[hh] = pv
                    r_new = block_sum
                else:
                    r_old = r_ref[hh]
                    acc_ref[hh] += jnp.exp2(r_old) * pv
                    r_new = r_old + block_sum
                r_ref[hh] = r_new
                alive = r_new if alive is None else jnp.maximum(alive, r_new)
        return (jnp.max(alive) > LOG2_WEIGHT_FLOOR).astype(jnp.int32)

    go = step(i, True)

    def cond(c):
        n, go = c
        return jnp.logical_and(n < i, go > 0)

    def body(c):
        n, _ = c
        return n + 1, step(i - 1 - n, False)

    lax.while_loop(cond, body, (jnp.int32(0), go))
    for hh in range(heads):
        o_ref[:, lanes[hh]] = acc_ref[hh].astype(o_ref.dtype)


def _attention(proj3):
    b, s, _ = proj3.shape
    width = ATT_HEADS_PER_STEP * ATT_HEAD_DIM
    grid = (b, ATT_HEADS // ATT_HEADS_PER_STEP, s // ATT_TILE)
    qb, kb, vb = Q_OFF // width, K_OFF // width, V_OFF // width
    return pl.pallas_call(
        _attn_kernel,
        grid=grid,
        in_specs=[
            pl.BlockSpec((None, ATT_TILE, width), lambda bi, h, i: (bi, i, qb + h)),
            pl.BlockSpec((None, s, width), lambda bi, h, i: (bi, 0, kb + h)),
            pl.BlockSpec((None, s, width), lambda bi, h, i: (bi, 0, vb + h)),
        ],
        out_specs=pl.BlockSpec((None, ATT_TILE, width), lambda bi, h, i: (bi, i, h)),
        out_shape=jax.ShapeDtypeStruct((b, s, ATT_WIDTH), BF16),
        scratch_shapes=[
            pltpu.VMEM((ATT_HEADS_PER_STEP, ATT_TILE, 1), F32),
            pltpu.VMEM((ATT_HEADS_PER_STEP, ATT_TILE, ATT_HEAD_DIM), F32),
        ],
        compiler_params=_params("arbitrary", "arbitrary", "arbitrary"),
        name="attn",
    )(proj3, proj3, proj3)


def _split3(x):
    h1 = x.astype(BF16)
    r1 = x - h1.astype(F32)
    h2 = r1.astype(BF16)
    h3 = (r1 - h2.astype(F32)).astype(BF16)
    return h1, h2, h3


def _dot3(x, m, lhs=True):
    parts = _split3(x)
    if lhs:
        outs = [jnp.dot(p, m, preferred_element_type=F32) for p in parts]
    else:
        outs = [jnp.dot(m, p, preferred_element_type=F32) for p in parts]
    return outs[0] + outs[1] + outs[2]


SSD_CHUNKS_PER_STEP = 4


def _ssd_chunk(first, z, cur, prev, dt_raw, cw_ref, cb_ref, dtb_ref, alog_ref, dskip_ref, gain_ref,
               expand_ref, state_ref):
    L = SSD_CHUNK
    out_t = lax.broadcasted_iota(jnp.int32, ((CONV_WIDTH - 1) * L, 2 * L), 0)
    src_t = lax.broadcasted_iota(jnp.int32, ((CONV_WIDTH - 1) * L, 2 * L), 1)
    shift = out_t // L + 1
    hit = src_t == L + (out_t - (shift - 1) * L) - shift
    shifts = jnp.logical_and(hit, jnp.logical_or(src_t >= L, jnp.logical_not(first))).astype(BF16)
    pieces = []
    col = 0
    for cur_blk, prev_blk in zip(cur, prev):
        width = cur_blk.shape[1]
        both = jnp.concatenate([prev_blk, cur_blk], axis=0)
        back = jnp.dot(shifts, both, preferred_element_type=F32)
        acc = cb_ref[:, col:col + width] + cur_blk.astype(F32) * cw_ref[CONV_WIDTH - 1:CONV_WIDTH, col:col + width]
        for d in range(CONV_WIDTH - 1):
            tap = CONV_WIDTH - 2 - d
            acc = acc + back[d * L:(d + 1) * L] * cw_ref[tap:tap + 1, col:col + width]
        pieces.append(_silu(acc))
        col += width
    xs = jnp.concatenate(pieces[0:2], axis=1)
    b_in = pieces[2][:, 0:SSM_GROUPS * SSM_STATE]
    c_in = pieces[2][:, SSM_GROUPS * SSM_STATE:2 * SSM_GROUPS * SSM_STATE]

    lane = lax.broadcasted_iota(jnp.int32, (1, LANES), 1)
    a2 = jnp.where(lane < SSM_HEADS, -LOG2_E * jnp.exp(alog_ref[...]), 0.0)
    dt_pre = dt_raw + dtb_ref[...]
    dt = jnp.maximum(dt_pre, 0.0) + jnp.log(1.0 + jnp.exp(-jnp.abs(dt_pre)))
    da = dt * a2
    rows = lax.broadcasted_iota(jnp.int32, (L, L), 0)
    cols = lax.broadcasted_iota(jnp.int32, (L, L), 1)
    causal = rows >= cols
    not_causal = jnp.where(causal, 0.0, -jnp.inf)
    a_cum = _dot3(da, causal.astype(BF16), lhs=False)
    src_side = (a_cum - jnp.log2(dt)).T
    a_last = a_cum[L - 1:L, :]
    expand = expand_ref[...]
    decay_in = jnp.dot(jnp.exp2(a_cum).astype(BF16), expand, preferred_element_type=F32)
    w_end = jnp.dot((jnp.exp2(a_last - a_cum) * dt).astype(BF16), expand, preferred_element_type=F32)
    chunk_decay = _dot3(jnp.broadcast_to(jnp.exp2(a_last), (SUBLANES, LANES)), expand)[0:1, :]

    xw = (xs * w_end).astype(BF16)
    lane_in_pair = lax.broadcasted_iota(jnp.int32, (L, LANES), 1)
    y_parts = []
    for g in range(SSM_GROUPS):
        bg = b_in[:, g * SSM_STATE:(g + 1) * SSM_STATE]
        cg = c_in[:, g * SSM_STATE:(g + 1) * SSM_STATE].astype(BF16)
        cb = lax.dot_general(cg, bg.astype(BF16), (((1,), (1,)), ((), ())),
                             preferred_element_type=F32)
        gcols = slice(g * GROUP_WIDTH, (g + 1) * GROUP_WIDTH)
        st = state_ref[g]
        y_off = jnp.dot(cg, st.astype(BF16), preferred_element_type=F32) * decay_in[:, gcols]
        st_new = jnp.dot(bg.T.astype(BF16), xw[:, gcols], preferred_element_type=F32)
        state_ref[g] = st * chunk_decay[:, gcols] + st_new
        heads_per_group = SSM_HEADS // SSM_GROUPS
        for pair in range(heads_per_group // 2):
            h0 = g * heads_per_group + 2 * pair
            slab = xs[:, h0 * SSM_HEAD_DIM:(h0 + 2) * SSM_HEAD_DIM]
            mats = []
            for hh in (h0, h0 + 1):
                seg = (a_cum[:, hh:hh + 1] - src_side[hh:hh + 1, :]) + not_causal
                mats.append((cb * jnp.exp2(seg)).astype(BF16))
            lhs = jnp.concatenate(mats, axis=1)
            rhs = jnp.concatenate(
                [jnp.where(lane_in_pair < SSM_HEAD_DIM, slab, 0.0),
                 jnp.where(lane_in_pair >= SSM_HEAD_DIM, slab, 0.0)], axis=0).astype(BF16)
            y_diag = jnp.dot(lhs, rhs, preferred_element_type=F32)
            lo = (h0 - g * heads_per_group) * SSM_HEAD_DIM
            y_parts.append(y_diag + y_off[:, lo:lo + LANES])
    y = jnp.concatenate(y_parts, axis=1) + xs * dskip_ref[...]
    y = y * _silu(z.astype(F32))
    outs = []
    for g in range(SSM_GROUPS):
        yg = y[:, g * GROUP_WIDTH:(g + 1) * GROUP_WIDTH]
        ms = jnp.mean(yg * yg, axis=-1, keepdims=True)
        outs.append(yg * lax.rsqrt(ms + EPS))
    return (jnp.concatenate(outs, axis=1) * gain_ref[...]).astype(BF16)


def _ssd_kernel(z_ref, x0_ref, x1_ref, bc_ref, p0_ref, p1_ref, pbc_ref, dt_ref, cw_ref, cb_ref, dtb_ref,
                alog_ref, dskip_ref, gain_ref, expand_ref, wo_ref, wg_ref, wu_ref, wd_ref,
                o_ref, wo_out, wg_out, wu_out, wd_out, state_ref):
    c = pl.program_id(1)
    L = SSD_CHUNK

    for src, dst in ((wo_ref, wo_out), (wg_ref, wg_out), (wu_ref, wu_out), (wd_ref, wd_out)):
        dst[...] = src[...].astype(BF16)

    @pl.when(c == 0)
    def _():
        state_ref[...] = jnp.zeros_like(state_ref)

    cur_refs = (x0_ref, x1_ref, bc_ref)
    prev = tuple(r[...] for r in (p0_ref, p1_ref, pbc_ref))
    first = c == 0
    for k in range(SSD_CHUNKS_PER_STEP):
        rows = slice(k * L, (k + 1) * L)
        cur = tuple(r[rows, :] for r in cur_refs)
        o_ref[rows, :] = _ssd_chunk(first, z_ref[rows, :], cur, prev, dt_ref[rows, :], cw_ref, cb_ref, dtb_ref,
                                    alog_ref, dskip_ref, gain_ref, expand_ref, state_ref)
        prev = cur
        first = jnp.bool_(False)


def _ssd(proj3, dt3, conv_w, conv_b, dt_bias, a_log, d_skip_x, gain, expand, layer, w_out, w_gate, w_up, w_down):
    b, s, _ = proj3.shape
    L = SSD_CHUNK
    per_step = SSD_CHUNKS_PER_STEP if s % (SSD_CHUNKS_PER_STEP * L) == 0 else 1
    assert per_step == SSD_CHUNKS_PER_STEP
    rows = per_step * L
    half = SSM_WIDTH // 2
    grid = (b, s // rows)
    const = lambda shape: pl.BlockSpec(shape, lambda bi, c: (0, 0))
    xbc = lambda k: pl.BlockSpec((None, rows, half), lambda bi, c: (bi, c, XBC_OFF // half + k))
    xbc_prev = lambda k: pl.BlockSpec(
        (None, L, half), lambda bi, c: (bi, jnp.maximum(c * per_step - 1, 0), XBC_OFF // half + k))
    steps = grid[0] * grid[1]

    def slab(w, out):
        rows_total, cols = w.shape[1], w.shape[2]
        n = steps if rows_total % (steps * BF16_SUBLANES) == 0 else steps // 2
        assert rows_total % (n * BF16_SUBLANES) == 0
        per = steps // n
        return pl.BlockSpec((None, rows_total // n, cols),
                            lambda bi, c: ((0 if out else layer), (bi * grid[1] + c) // per, 0))

    weights = (w_out, w_gate, w_up, w_down)
    return pl.pallas_call(
        _ssd_kernel,
        grid=grid,
        in_specs=[
            pl.BlockSpec((None, rows, SSM_WIDTH), lambda bi, c: (bi, c, Z_OFF // SSM_WIDTH)),
            xbc(0), xbc(1), xbc(2),
            xbc_prev(0), xbc_prev(1), xbc_prev(2),
            pl.BlockSpec((None, rows, LANES), lambda bi, c: (bi, c, 0)),
            const((CONV_WIDTH, CONV_DIM)),
            const((1, CONV_DIM)),
            const((1, LANES)),
            const((1, LANES)),
            const((1, SSM_WIDTH)),
            const((1, SSM_WIDTH)),
            const((LANES, SSM_WIDTH)),
        ] + [slab(w, False) for w in weights],
        out_specs=[pl.BlockSpec((None, rows, SSM_WIDTH), lambda bi, c: (bi, c, 0))]
        + [slab(w, True) for w in weights],
        out_shape=[jax.ShapeDtypeStruct((b, s, SSM_WIDTH), BF16)]
        + [jax.ShapeDtypeStruct((1,) + w.shape[1:], BF16) for w in weights],
        scratch_shapes=[pltpu.VMEM((SSM_GROUPS, SSM_STATE, GROUP_WIDTH), F32)],
        compiler_params=_params("arbitrary", "arbitrary"),
        name="ssd",
    )(proj3, proj3, proj3, proj3, proj3, proj3, proj3, dt3, conv_w, conv_b, dt_bias, a_log, d_skip_x,
      gain, expand, *weights)


OUTPROJ_ROW_GROUPS = 2


def _outproj_kernel(x_ref, oa_ref, os_ref, g_ref, w_ref, o_ref):
    rows = x_ref.shape[0] // OUTPROJ_ROW_GROUPS
    for r in range(OUTPROJ_ROW_GROUPS):
        sl = slice(r * rows, (r + 1) * rows)
        a = oa_ref[sl, :].astype(F32)
        ms = jnp.mean(a * a, axis=-1, keepdims=True)
        an = (a * lax.rsqrt(ms + EPS) * g_ref[...]).astype(BF16)
        y = jnp.dot(an, w_ref[0:ATT_WIDTH, :], preferred_element_type=F32)
        y = y + jnp.dot(os_ref[sl, :], w_ref[ATT_WIDTH:ATT_WIDTH + SSM_WIDTH, :], preferred_element_type=F32)
        o_ref[sl, :] = x_ref[sl, :] + y


def _out_proj(x2, o_att, o_ssm, gain, w_out, layer, *, tm):
    t = x2.shape[0]
    return pl.pallas_call(
        _outproj_kernel,
        grid=(t // tm,),
        in_specs=[
            pl.BlockSpec((tm, D_MODEL), lambda i: (i, 0)),
            pl.BlockSpec((tm, ATT_WIDTH), lambda i: (i, 0)),
            pl.BlockSpec((tm, SSM_WIDTH), lambda i: (i, 0)),
            pl.BlockSpec((1, ATT_WIDTH), lambda i: (0, 0)),
            pl.BlockSpec((None, ATT_WIDTH + SSM_WIDTH, D_MODEL), lambda i: (layer, 0, 0),
                         pipeline_mode=pl.Buffered(1)),
        ],
        out_specs=pl.BlockSpec((tm, D_MODEL), lambda i: (i, 0)),
        out_shape=jax.ShapeDtypeStruct((t, D_MODEL), F32),
        compiler_params=_params("arbitrary"),
        name="out_proj",
    )(x2, o_att, o_ssm, gain, w_out)


FFN_FIRST_STEP_ROW_GROUPS = 2


def _ffn_kernel(x_ref, g_ref, wg_ref, wu_ref, wd_ref, o_ref, h_ref):
    f = pl.program_id(1)

    def swiglu(h):
        gate = jnp.dot(h, wg_ref[...], preferred_element_type=F32)
        up = jnp.dot(h, wu_ref[...], preferred_element_type=F32)
        act = (_silu(gate) * up).astype(BF16)
        return jnp.dot(act, wd_ref[...], preferred_element_type=F32)

    @pl.when(f == 0)
    def _():
        rows = x_ref.shape[0] // FFN_FIRST_STEP_ROW_GROUPS
        for r in range(FFN_FIRST_STEP_ROW_GROUPS):
            sl = slice(r * rows, (r + 1) * rows)
            x = x_ref[sl, :]
            ms = jnp.mean(x * x, axis=-1, keepdims=True)
            h = (x * lax.rsqrt(ms + EPS) * g_ref[...]).astype(BF16)
            h_ref[sl, :] = h
            o_ref[sl, :] = x + swiglu(h)

    @pl.when(f > 0)
    def _():
        o_ref[...] += swiglu(h_ref[...])


def _ffn(x2, g, w_gate, w_up, w_down, layer, *, tm, tf):
    t = x2.shape[0]
    return pl.pallas_call(
        _ffn_kernel,
        grid=(t // tm, D_FF // tf),
        in_specs=[
            pl.BlockSpec((tm, D_MODEL), lambda i, f: (i, 0)),
            pl.BlockSpec((1, D_MODEL), lambda i, f: (0, 0)),
            pl.BlockSpec((None, D_MODEL, tf), lambda i, f: (layer, 0, f)),
            pl.BlockSpec((None, D_MODEL, tf), lambda i, f: (layer, 0, f)),
            pl.BlockSpec((None, tf, D_MODEL), lambda i, f: (layer, f, 0)),
        ],
        out_specs=pl.BlockSpec((tm, D_MODEL), lambda i, f: (i, 0)),
        out_shape=jax.ShapeDtypeStruct((t, D_MODEL), F32),
        scratch_shapes=[pltpu.VMEM((tm, D_MODEL), BF16)],
        compiler_params=_params("arbitrary", "arbitrary"),
        name="ffn",
    )(x2, g, w_gate, w_up, w_down)


def _tile(pref, n):
    return pref if n % pref == 0 else n


def _layer(x, layer, norm_mix, q_gain, k_gain, conv_w, conv_b, dt_bias, a_log, d_skip,
           attn_out_gain, ssm_out_gain, norm_ffn, w_main, w_dt, w_out, w_gate, w_up, w_down, expand):
    b, s, _ = x.shape
    t = b * s
    x2 = x.reshape(t, D_MODEL)
    row = lambda v: v.reshape(1, -1).astype(F32)
    pad_lanes = lambda v: jnp.pad(v.astype(F32), (0, LANES - v.shape[0])).reshape(1, LANES)

    proj, dt_raw = _in_proj(x2, row(norm_mix), w_main, w_dt, layer,
                            row(q_gain) * (LOG2_E * ATT_HEAD_DIM ** -0.5), row(k_gain),
                            tm=_tile(512, t), tn=MAIN_DIM // 2)
    proj3 = proj.reshape(b, s, MAIN_DIM)

    o_att = _attention(proj3)
    o_ssm, w_out, w_gate, w_up, w_down = _ssd(
        proj3, dt_raw.reshape(b, s, LANES), conv_w.astype(F32), row(conv_b), pad_lanes(dt_bias), pad_lanes(a_log),
        row(jnp.repeat(d_skip, SSM_HEAD_DIM)), row(ssm_out_gain), expand, layer, w_out, w_gate, w_up, w_down)

    x2 = _out_proj(x2, o_att.reshape(t, ATT_WIDTH), o_ssm.reshape(t, SSM_WIDTH),
                   row(attn_out_gain), w_out, 0, tm=_tile(1024, t))
    x2 = _ffn(x2, row(norm_ffn), w_gate, w_up, w_down, 0, tm=_tile(1024, t), tf=512)
    return x2.reshape(b, s, D_MODEL)


def _inproj_weights(w_in):
    w_main = w_in.astype(BF16)
    w_dt = jnp.pad(w_in[:, :, MAIN_DIM:], ((0, 0), (0, 0), (0, LANES - SSM_HEADS))).astype(BF16)
    return w_main, w_dt


def _head_expand():
    head_of_lane = jnp.arange(SSM_WIDTH, dtype=jnp.int32) // SSM_HEAD_DIM
    return (jnp.arange(LANES, dtype=jnp.int32)[:, None] == head_of_lane[None, :]).astype(BF16)


def kernel(x, norm_mix, w_in, q_gain, k_gain, conv_w, conv_b, dt_bias, a_log, d_skip, attn_out_gain, ssm_out_gain, w_out, norm_ffn, w_gate, w_up, w_down):
    weights = _inproj_weights(w_in) + (w_out, w_gate, w_up, w_down)
    expand = _head_expand()
    for i in range(norm_mix.shape[0]):
        x = _layer(x, i, norm_mix[i], q_gain[i], k_gain[i], conv_w[i], conv_b[i], dt_bias[i], a_log[i],
                   d_skip[i], attn_out_gain[i], ssm_out_gain[i], norm_ffn[i], *weights, expand)
    return x
```

```python
import functools

import jax
import jax.numpy as jnp
from jax import lax
from jax.experimental import pallas as pl
from jax.experimental.pallas import tpu as pltpu

F32 = jnp.float32
BF16 = jnp.bfloat16

D_MODEL = 2048
ATT_WIDTH = 1024
SSM_WIDTH = 1024
ATT_HEAD_DIM = 128
ATT_HEADS = ATT_WIDTH // ATT_HEAD_DIM
SSM_HEAD_DIM = 64
SSM_HEADS = SSM_WIDTH // SSM_HEAD_DIM
SSM_GROUPS = 2
SSM_STATE = 128
GROUP_WIDTH = SSM_WIDTH // SSM_GROUPS
CONV_WIDTH = 4
CONV_DIM = SSM_WIDTH + 2 * SSM_GROUPS * SSM_STATE
SSD_CHUNK = 128
D_FF = 5632
EPS = 1e-6
LOG2_E = 1.4426950408889634

LANES = 128
SUBLANES = 8
BF16_SUBLANES = 16
VMEM_LIMIT = 56 * 1024 * 1024

MAIN_DIM = 3 * ATT_WIDTH + SSM_WIDTH + CONV_DIM
Q_OFF, K_OFF, V_OFF = 0, ATT_WIDTH, 2 * ATT_WIDTH
Z_OFF = 3 * ATT_WIDTH
XBC_OFF = Z_OFF + SSM_WIDTH


def _params(*sem):
    return pltpu.CompilerParams(dimension_semantics=sem, vmem_limit_bytes=VMEM_LIMIT)


def _silu(x):
    h = 0.5 * x
    return h + h * jnp.tanh(h)


INPROJ_SUB = 512


def _inproj_kernel(x_ref, g_ref, w_ref, wdt_ref, qg_ref, kg_ref, o_ref, dt_ref, h_ref, *, tn):
    j = pl.program_id(1)
    assert tn >= K_OFF + ATT_WIDTH

    def columns(first_tile):
        h = h_ref[...]
        for lo in range(0, tn, INPROJ_SUB):
            hi = min(lo + INPROJ_SUB, tn)
            acc = jnp.dot(h, w_ref[:, lo:hi], preferred_element_type=F32)
            for c0 in range(lo, hi, ATT_HEAD_DIM):
                seg = acc[:, c0 - lo:c0 - lo + ATT_HEAD_DIM]
                if first_tile and c0 < K_OFF + ATT_WIDTH:
                    gain = qg_ref[...] if c0 < K_OFF else kg_ref[...]
                    ms = jnp.mean(seg * seg, axis=-1, keepdims=True)
                    seg = seg * lax.rsqrt(ms + EPS) * gain
                o_ref[:, c0:c0 + ATT_HEAD_DIM] = seg.astype(BF16)

    @pl.when(j == 0)
    def _():
        x = x_ref[...]
        ms = jnp.mean(x * x, axis=-1, keepdims=True)
        h = (x * lax.rsqrt(ms + EPS) * g_ref[...]).astype(BF16)
        h_ref[...] = h
        dt_ref[...] = jnp.dot(h, wdt_ref[...], preferred_element_type=F32)
        columns(True)

    @pl.when(j > 0)
    def _():
        columns(False)


def _in_proj(x2, g, w_main, w_dt, layer, qg, kg, *, tm, tn):
    t = x2.shape[0]
    grid = (t // tm, MAIN_DIM // tn)
    return pl.pallas_call(
        functools.partial(_inproj_kernel, tn=tn),
        grid=grid,
        in_specs=[
            pl.BlockSpec((tm, D_MODEL), lambda i, j: (i, 0)),
            pl.BlockSpec((1, D_MODEL), lambda i, j: (0, 0)),
            pl.BlockSpec((None, D_MODEL, tn), lambda i, j: (0, 0, j)),
            pl.BlockSpec((None, D_MODEL, LANES), lambda i, j: (layer, 0, 0)),
            pl.BlockSpec((1, ATT_HEAD_DIM), lambda i, j: (0, 0)),
            pl.BlockSpec((1, ATT_HEAD_DIM), lambda i, j: (0, 0)),
        ],
        out_specs=[
            pl.BlockSpec((tm, tn), lambda i, j: (i, j)),
            pl.BlockSpec((tm, LANES), lambda i, j: (i, 0)),
        ],
        out_shape=[
            jax.ShapeDtypeStruct((t, MAIN_DIM), BF16),
            jax.ShapeDtypeStruct((t, LANES), F32),
        ],
        scratch_shapes=[pltpu.VMEM((tm, D_MODEL), BF16)],
        compiler_params=_params("arbitrary", "arbitrary"),
        name="in_proj",
    )(x2, g, w_main, w_dt, qg, kg)


ATT_TILE = 256
ATT_HEADS_PER_STEP = 8
LOG2_WEIGHT_FLOOR = -152.0


def _attn_kernel(q_ref, k_ref, v_ref, o_ref, r_ref, acc_ref):
    t = ATT_TILE
    heads = ATT_HEADS_PER_STEP
    i = pl.program_id(2)
    rows = lax.broadcasted_iota(jnp.int32, (t, t), 0)
    cols = lax.broadcasted_iota(jnp.int32, (t, t), 1)
    later_sum = (rows > cols).astype(BF16)
    causal = cols < rows
    sign_bit = jnp.uint32(0x80000000)
    lanes = [slice(hh * ATT_HEAD_DIM, (hh + 1) * ATT_HEAD_DIM) for hh in range(heads)]

    def step(jblk, first):
        ks = pl.multiple_of(jblk * t, t)
        log_betas, log_remains, ws = [None] * heads, [None] * heads, [None] * heads
        alive = None
        for s in range(heads + 2):
            if s < heads:
                z = lax.dot_general(q_ref[:, lanes[s]], k_ref[pl.ds(ks, t), lanes[s]],
                                    (((1,), (1,)), ((), ())), preferred_element_type=F32)
                neg_abs = lax.bitcast_convert_type(lax.bitcast_convert_type(z, jnp.uint32) | sign_bit, F32)
                log_beta = jnp.minimum(z, 0.0) - jnp.log2(1.0 + jnp.exp2(neg_abs))
                log_remain = log_beta - z
                if first:
                    log_remain = jnp.where(causal, log_remain, 0.0)
                log_betas[s], log_remains[s] = log_beta, log_remain
            if 0 <= s - 1 < heads:
                hh = s - 1
                within = jnp.dot(log_remains[hh].astype(BF16), later_sum, preferred_element_type=F32)
                w = jnp.exp2(log_betas[hh] + within)
                if first:
                    w = jnp.where(causal, w, 0.0)
                ws[hh] = w.astype(BF16)
            if 0 <= s - 2 < heads:
                hh = s - 2
                pv = jnp.dot(ws[hh], v_ref[pl.ds(ks, t), lanes[hh]], preferred_element_type=F32)
                block_sum = jnp.sum(log_remains[hh], axis=1, keepdims=True)
                if first:
                    acc_ref[hh] = pv
                    r_new = block_sum
                else:
                    r_old = r_ref[hh]
                    acc_ref[hh] += jnp.exp2(r_old) * pv
                    r_new = r_old + block_sum
                r_ref[hh] = r_new
                alive = r_new if alive is None else jnp.maximum(alive, r_new)
        return (jnp.max(alive) > LOG2_WEIGHT_FLOOR).astype(jnp.int32)

    go = step(i, True)

    def cond(c):
        n, go = c
        return jnp.logical_and(n < i, go > 0)

    def body(c):
        n, _ = c
        return n + 1, step(i - 1 - n, False)

    lax.while_loop(cond, body, (jnp.int32(0), go))
    for hh in range(heads):
        o_ref[:, lanes[hh]] = acc_ref[hh].astype(o_ref.dtype)


def _attention(proj3):
    b, s, _ = proj3.shape
    width = ATT_HEADS_PER_STEP * ATT_HEAD_DIM
    grid = (b, ATT_HEADS // ATT_HEADS_PER_STEP, s // ATT_TILE)
    qb, kb, vb = Q_OFF // width, K_OFF // width, V_OFF // width
    return pl.pallas_call(
        _attn_kernel,
        grid=grid,
        in_specs=[
            pl.BlockSpec((None, ATT_TILE, width), lambda bi, h, i: (bi, i, qb + h)),
            pl.BlockSpec((None, s, width), lambda bi, h, i: (bi, 0, kb + h)),
            pl.BlockSpec((None, s, width), lambda bi, h, i: (bi, 0, vb + h)),
        ],
        out_specs=pl.BlockSpec((None, ATT_TILE, width), lambda bi, h, i: (bi, i, h)),
        out_shape=jax.ShapeDtypeStruct((b, s, ATT_WIDTH), BF16),
        scratch_shapes=[
            pltpu.VMEM((ATT_HEADS_PER_STEP, ATT_TILE, 1), F32),
            pltpu.VMEM((ATT_HEADS_PER_STEP, ATT_TILE, ATT_HEAD_DIM), F32),
        ],
        compiler_params=_params("arbitrary", "arbitrary", "arbitrary"),
        name="attn",
    )(proj3, proj3, proj3)


def _split3(x):
    h1 = x.astype(BF16)
    r1 = x - h1.astype(F32)
    h2 = r1.astype(BF16)
    h3 = (r1 - h2.astype(F32)).astype(BF16)
    return h1, h2, h3


def _dot3(x, m, lhs=True):
    parts = _split3(x)
    if lhs:
        outs = [jnp.dot(p, m, preferred_element_type=F32) for p in parts]
    else:
        outs = [jnp.dot(m, p, preferred_element_type=F32) for p in parts]
    return outs[0] + outs[1] + outs[2]


SSD_CHUNKS_PER_STEP = 4


def _ssd_chunk(first, z, cur, prev, dt_raw, cw_ref, cb_ref, dtb_ref, alog_ref, dskip_ref, gain_ref,
               expand_ref, state_ref):
    L = SSD_CHUNK
    out_t = lax.broadcasted_iota(jnp.int32, ((CONV_WIDTH - 1) * L, 2 * L), 0)
    src_t = lax.broadcasted_iota(jnp.int32, ((CONV_WIDTH - 1) * L, 2 * L), 1)
    shift = out_t // L + 1
    hit = src_t == L + (out_t - (shift - 1) * L) - shift
    shifts = jnp.logical_and(hit, jnp.logical_or(src_t >= L, jnp.logical_not(first))).astype(BF16)
    pieces = []
    col = 0
    for cur_blk, prev_blk in zip(cur, prev):
        width = cur_blk.shape[1]
        both = jnp.concatenate([prev_blk, cur_blk], axis=0)
        back = jnp.dot(shifts, both, preferred_element_type=F32)
        acc = cb_ref[:, col:col + width] + cur_blk.astype(F32) * cw_ref[CONV_WIDTH - 1:CONV_WIDTH, col:col + width]
        for d in range(CONV_WIDTH - 1):
            tap = CONV_WIDTH - 2 - d
            acc = acc + back[d * L:(d + 1) * L] * cw_ref[tap:tap + 1, col:col + width]
        pieces.append(_silu(acc))
        col += width
    xs = jnp.concatenate(pieces[0:2], axis=1)
    b_in = pieces[2][:, 0:SSM_GROUPS * SSM_STATE]
    c_in = pieces[2][:, SSM_GROUPS * SSM_STATE:2 * SSM_GROUPS * SSM_STATE]

    lane = lax.broadcasted_iota(jnp.int32, (1, LANES), 1)
    a2 = jnp.where(lane < SSM_HEADS, -LOG2_E * jnp.exp(alog_ref[...]), 0.0)
    dt_pre = dt_raw + dtb_ref[...]
    dt = jnp.maximum(dt_pre, 0.0) + jnp.log(1.0 + jnp.exp(-jnp.abs(dt_pre)))
    da = dt * a2
    rows = lax.broadcasted_iota(jnp.int32, (L, L), 0)
    cols = lax.broadcasted_iota(jnp.int32, (L, L), 1)
    causal = rows >= cols
    not_causal = jnp.where(causal, 0.0, -jnp.inf)
    a_cum = _dot3(da, causal.astype(BF16), lhs=False)
    src_side = (a_cum - jnp.log2(dt)).T
    a_last = a_cum[L - 1:L, :]
    expand = expand_ref[...]
    decay_in = jnp.dot(jnp.exp2(a_cum).astype(BF16), expand, preferred_element_type=F32)
    w_end = jnp.dot((jnp.exp2(a_last - a_cum) * dt).astype(BF16), expand, preferred_element_type=F32)
    chunk_decay = _dot3(jnp.broadcast_to(jnp.exp2(a_last), (SUBLANES, LANES)), expand)[0:1, :]

    xw = (xs * w_end).astype(BF16)
    lane_in_pair = lax.broadcasted_iota(jnp.int32, (L, LANES), 1)
    y_parts = []
    for g in range(SSM_GROUPS):
        bg = b_in[:, g * SSM_STATE:(g + 1) * SSM_STATE]
        cg = c_in[:, g * SSM_STATE:(g + 1) * SSM_STATE].astype(BF16)
        cb = lax.dot_general(cg, bg.astype(BF16), (((1,), (1,)), ((), ())),
                             preferred_element_type=F32)
        gcols = slice(g * GROUP_WIDTH, (g + 1) * GROUP_WIDTH)
        st = state_ref[g]
        y_off = jnp.dot(cg, st.astype(BF16), preferred_element_type=F32) * decay_in[:, gcols]
        st_new = jnp.dot(bg.T.astype(BF16), xw[:, gcols], preferred_element_type=F32)
        state_ref[g] = st * chunk_decay[:, gcols] + st_new
        heads_per_group = SSM_HEADS // SSM_GROUPS
        for pair in range(heads_per_group // 2):
            h0 = g * heads_per_group + 2 * pair
            slab = xs[:, h0 * SSM_HEAD_DIM:(h0 + 2) * SSM_HEAD_DIM]
            mats = []
            for hh in (h0, h0 + 1):
                seg = (a_cum[:, hh:hh + 1] - src_side[hh:hh + 1, :]) + not_causal
                mats.append((cb * jnp.exp2(seg)).astype(BF16))
            lhs = jnp.concatenate(mats, axis=1)
            rhs = jnp.concatenate(
                [jnp.where(lane_in_pair < SSM_HEAD_DIM, slab, 0.0),
                 jnp.where(lane_in_pair >= SSM_HEAD_DIM, slab, 0.0)], axis=0).astype(BF16)
            y_diag = jnp.dot(lhs, rhs, preferred_element_type=F32)
            lo = (h0 - g * heads_per_group) * SSM_HEAD_DIM
            y_parts.append(y_diag + y_off[:, lo:lo + LANES])
    y = jnp.concatenate(y_parts, axis=1) + xs * dskip_ref[...]
    y = y * _silu(z.astype(F32))
    outs = []
    for g in range(SSM_GROUPS):
        yg = y[:, g * GROUP_WIDTH:(g + 1) * GROUP_WIDTH]
        ms = jnp.mean(yg * yg, axis=-1, keepdims=True)
        outs.append(yg * lax.rsqrt(ms + EPS))
    return (jnp.concatenate(outs, axis=1) * gain_ref[...]).astype(BF16)


def _ssd_kernel(z_ref, x0_ref, x1_ref, bc_ref, p0_ref, p1_ref, pbc_ref, dt_ref, cw_ref, cb_ref, dtb_ref,
                alog_ref, dskip_ref, gain_ref, expand_ref, *rest):
    n_weights = (len(rest) - 2) // 2
    w_refs, o_ref, w_outs, state_ref = (rest[:n_weights], rest[n_weights], rest[n_weights + 1:-1], rest[-1])
    c = pl.program_id(1)
    L = SSD_CHUNK

    for src, dst in zip(w_refs, w_outs):
        dst[...] = src[...].astype(BF16)

    @pl.when(c == 0)
    def _():
        state_ref[...] = jnp.zeros_like(state_ref)

    cur_refs = (x0_ref, x1_ref, bc_ref)
    prev = tuple(r[...] for r in (p0_ref, p1_ref, pbc_ref))
    first = c == 0
    for k in range(SSD_CHUNKS_PER_STEP):
        rows = slice(k * L, (k + 1) * L)
        cur = tuple(r[rows, :] for r in cur_refs)
        o_ref[rows, :] = _ssd_chunk(first, z_ref[rows, :], cur, prev, dt_ref[rows, :], cw_ref, cb_ref, dtb_ref,
                                    alog_ref, dskip_ref, gain_ref, expand_ref, state_ref)
        prev = cur
        first = jnp.bool_(False)


def _ssd(proj3, dt3, conv_w, conv_b, dt_bias, a_log, d_skip_x, gain, expand, weights):
    b, s, _ = proj3.shape
    L = SSD_CHUNK
    per_step = SSD_CHUNKS_PER_STEP if s % (SSD_CHUNKS_PER_STEP * L) == 0 else 1
    assert per_step == SSD_CHUNKS_PER_STEP
    rows = per_step * L
    half = SSM_WIDTH // 2
    grid = (b, s // rows)
    const = lambda shape: pl.BlockSpec(shape, lambda bi, c: (0, 0))
    xbc = lambda k: pl.BlockSpec((None, rows, half), lambda bi, c: (bi, c, XBC_OFF // half + k))
    xbc_prev = lambda k: pl.BlockSpec(
        (None, L, half), lambda bi, c: (bi, jnp.maximum(c * per_step - 1, 0), XBC_OFF // half + k))
    steps = grid[0] * grid[1]

    def slab(w, out):
        rows_total, cols = w.shape[1], w.shape[2]
        n = steps if rows_total % (steps * BF16_SUBLANES) == 0 else steps // 2
        assert rows_total % (n * BF16_SUBLANES) == 0
        per = steps // n
        return pl.BlockSpec((None, rows_total // n, cols),
                            lambda bi, c: (out, (bi * grid[1] + c) // per, 0))
    return pl.pallas_call(
        _ssd_kernel,
        grid=grid,
        in_specs=[
            pl.BlockSpec((None, rows, SSM_WIDTH), lambda bi, c: (bi, c, Z_OFF // SSM_WIDTH)),
            xbc(0), xbc(1), xbc(2),
            xbc_prev(0), xbc_prev(1), xbc_prev(2),
            pl.BlockSpec((None, rows, LANES), lambda bi, c: (bi, c, 0)),
            const((CONV_WIDTH, CONV_DIM)),
            const((1, CONV_DIM)),
            const((1, LANES)),
            const((1, LANES)),
            const((1, SSM_WIDTH)),
            const((1, SSM_WIDTH)),
            const((LANES, SSM_WIDTH)),
        ] + [slab(w, layer) for w, layer in weights],
        out_specs=[pl.BlockSpec((None, rows, SSM_WIDTH), lambda bi, c: (bi, c, 0))]
        + [slab(w, 0) for w, _ in weights],
        out_shape=[jax.ShapeDtypeStruct((b, s, SSM_WIDTH), BF16)]
        + [jax.ShapeDtypeStruct((1,) + w.shape[1:], BF16) for w, _ in weights],
        scratch_shapes=[pltpu.VMEM((SSM_GROUPS, SSM_STATE, GROUP_WIDTH), F32)],
        compiler_params=_params("arbitrary", "arbitrary"),
        name="ssd",
    )(proj3, proj3, proj3, proj3, proj3, proj3, proj3, dt3, conv_w, conv_b, dt_bias, a_log, d_skip_x,
      gain, expand, *[w for w, _ in weights])


OUTPROJ_ROW_GROUPS = 2


def _outproj_kernel(x_ref, oa_ref, os_ref, g_ref, w_ref, o_ref):
    rows = x_ref.shape[0] // OUTPROJ_ROW_GROUPS
    for r in range(OUTPROJ_ROW_GROUPS):
        sl = slice(r * rows, (r + 1) * rows)
        a = oa_ref[sl, :].astype(F32)
        ms = jnp.mean(a * a, axis=-1, keepdims=True)
        an = (a * lax.rsqrt(ms + EPS) * g_ref[...]).astype(BF16)
        y = jnp.dot(an, w_ref[0:ATT_WIDTH, :], preferred_element_type=F32)
        y = y + jnp.dot(os_ref[sl, :], w_ref[ATT_WIDTH:ATT_WIDTH + SSM_WIDTH, :], preferred_element_type=F32)
        o_ref[sl, :] = x_ref[sl, :] + y


def _out_proj(x2, o_att, o_ssm, gain, w_out, layer, *, tm):
    t = x2.shape[0]
    return pl.pallas_call(
        _outproj_kernel,
        grid=(t // tm,),
        in_specs=[
            pl.BlockSpec((tm, D_MODEL), lambda i: (i, 0)),
            pl.BlockSpec((tm, ATT_WIDTH), lambda i: (i, 0)),
            pl.BlockSpec((tm, SSM_WIDTH), lambda i: (i, 0)),
            pl.BlockSpec((1, ATT_WIDTH), lambda i: (0, 0)),
            pl.BlockSpec((None, ATT_WIDTH + SSM_WIDTH, D_MODEL), lambda i: (layer, 0, 0),
                         pipeline_mode=pl.Buffered(1)),
        ],
        out_specs=pl.BlockSpec((tm, D_MODEL), lambda i: (i, 0)),
        out_shape=jax.ShapeDtypeStruct((t, D_MODEL), F32),
        compiler_params=_params("arbitrary"),
        name="out_proj",
    )(x2, o_att, o_ssm, gain, w_out)


FFN_FIRST_STEP_ROW_GROUPS = 2


def _ffn_kernel(x_ref, g_ref, wg_ref, wu_ref, wd_ref, o_ref, h_ref):
    f = pl.program_id(1)

    def swiglu(h):
        gate = jnp.dot(h, wg_ref[...], preferred_element_type=F32)
        up = jnp.dot(h, wu_ref[...], preferred_element_type=F32)
        act = (_silu(gate) * up).astype(BF16)
        return jnp.dot(act, wd_ref[...], preferred_element_type=F32)

    @pl.when(f == 0)
    def _():
        rows = x_ref.shape[0] // FFN_FIRST_STEP_ROW_GROUPS
        for r in range(FFN_FIRST_STEP_ROW_GROUPS):
            sl = slice(r * rows, (r + 1) * rows)
            x = x_ref[sl, :]
            ms = jnp.mean(x * x, axis=-1, keepdims=True)
            h = (x * lax.rsqrt(ms + EPS) * g_ref[...]).astype(BF16)
            h_ref[sl, :] = h
            o_ref[sl, :] = x + swiglu(h)

    @pl.when(f > 0)
    def _():
        o_ref[...] += swiglu(h_ref[...])


def _ffn(x2, g, w_gate, w_up, w_down, layer, *, tm, tf):
    t = x2.shape[0]
    return pl.pallas_call(
        _ffn_kernel,
        grid=(t // tm, D_FF // tf),
        in_specs=[
            pl.BlockSpec((tm, D_MODEL), lambda i, f: (i, 0)),
            pl.BlockSpec((1, D_MODEL), lambda i, f: (0, 0)),
            pl.BlockSpec((None, D_MODEL, tf), lambda i, f: (layer, 0, f)),
            pl.BlockSpec((None, D_MODEL, tf), lambda i, f: (layer, 0, f)),
            pl.BlockSpec((None, tf, D_MODEL), lambda i, f: (layer, f, 0)),
        ],
        out_specs=pl.BlockSpec((tm, D_MODEL), lambda i, f: (i, 0)),
        out_shape=jax.ShapeDtypeStruct((t, D_MODEL), F32),
        scratch_shapes=[pltpu.VMEM((tm, D_MODEL), BF16)],
        compiler_params=_params("arbitrary", "arbitrary"),
        name="ffn",
    )(x2, g, w_gate, w_up, w_down)


def _tile(pref, n):
    return pref if n % pref == 0 else n


def _layer(x, layer, norm_mix, q_gain, k_gain, conv_w, conv_b, dt_bias, a_log, d_skip,
           attn_out_gain, ssm_out_gain, norm_ffn, w_main, w_dt, w_in, w_out, w_gate, w_up, w_down, expand):
    b, s, _ = x.shape
    t = b * s
    x2 = x.reshape(t, D_MODEL)
    row = lambda v: v.reshape(1, -1).astype(F32)
    pad_lanes = lambda v: jnp.pad(v.astype(F32), (0, LANES - v.shape[0])).reshape(1, LANES)

    proj, dt_raw = _in_proj(x2, row(norm_mix), w_main, w_dt, layer,
                            row(q_gain) * (LOG2_E * ATT_HEAD_DIM ** -0.5), row(k_gain),
                            tm=_tile(512, t), tn=MAIN_DIM // 2)
    proj3 = proj.reshape(b, s, MAIN_DIM)

    o_att = _attention(proj3)
    to_cast = [(w_out, layer), (w_gate, layer), (w_up, layer), (w_down, layer)]
    if layer + 1 < w_in.shape[0]:
        to_cast.append((w_in, layer + 1))
    o_ssm, w_out, w_gate, w_up, w_down, *w_main_next = _ssd(
        proj3, dt_raw.reshape(b, s, LANES), conv_w.astype(F32), row(conv_b), pad_lanes(dt_bias), pad_lanes(a_log),
        row(jnp.repeat(d_skip, SSM_HEAD_DIM)), row(ssm_out_gain), expand, to_cast)

    x2 = _out_proj(x2, o_att.reshape(t, ATT_WIDTH), o_ssm.reshape(t, SSM_WIDTH),
                   row(attn_out_gain), w_out, 0, tm=_tile(1024, t))
    x2 = _ffn(x2, row(norm_ffn), w_gate, w_up, w_down, 0, tm=_tile(1024, t), tf=512)
    return x2.reshape(b, s, D_MODEL), (w_main_next[0] if w_main_next else None)


def _inproj_weights(w_in):
    w_main0 = w_in[0:1].astype(BF16)
    w_dt = jnp.pad(w_in[:, :, MAIN_DIM:], ((0, 0), (0, 0), (0, LANES - SSM_HEADS))).astype(BF16)
    return w_main0, w_dt


def _head_expand():
    head_of_lane = jnp.arange(SSM_WIDTH, dtype=jnp.int32) // SSM_HEAD_DIM
    return (jnp.arange(LANES, dtype=jnp.int32)[:, None] == head_of_lane[None, :]).astype(BF16)


def kernel(x, norm_mix, w_in, q_gain, k_gain, conv_w, conv_b, dt_bias, a_log, d_skip, attn_out_gain, ssm_out_gain, w_out, norm_ffn, w_gate, w_up, w_down):
    w_main, w_dt = _inproj_weights(w_in)
    expand = _head_expand()
    for i in range(norm_mix.shape[0]):
        x, w_main = _layer(x, i, norm_mix[i], q_gain[i], k_gain[i], conv_w[i], conv_b[i], dt_bias[i], a_log[i],
                           d_skip[i], attn_out_gain[i], ssm_out_gain[i], norm_ffn[i], w_main, w_dt,
                           w_in, w_out, w_gate, w_up, w_down, expand)
    return x
```

```python
import functools

import jax
import jax.numpy as jnp
from jax import lax
from jax.experimental import pallas as pl
from jax.experimental.pallas import tpu as pltpu

F32 = jnp.float32
BF16 = jnp.bfloat16

D_MODEL = 2048
ATT_WIDTH = 1024
SSM_WIDTH = 1024
ATT_HEAD_DIM = 128
ATT_HEADS = ATT_WIDTH // ATT_HEAD_DIM
SSM_HEAD_DIM = 64
SSM_HEADS = SSM_WIDTH // SSM_HEAD_DIM
SSM_GROUPS = 2
SSM_STATE = 128
GROUP_WIDTH = SSM_WIDTH // SSM_GROUPS
CONV_WIDTH = 4
CONV_DIM = SSM_WIDTH + 2 * SSM_GROUPS * SSM_STATE
SSD_CHUNK = 128
D_FF = 5632
EPS = 1e-6
LOG2_E = 1.4426950408889634

LANES = 128
SUBLANES = 8
BF16_SUBLANES = 16
VMEM_LIMIT = 56 * 1024 * 1024

MAIN_DIM = 3 * ATT_WIDTH + SSM_WIDTH + CONV_DIM
Q_OFF, K_OFF, V_OFF = 0, ATT_WIDTH, 2 * ATT_WIDTH
Z_OFF = 3 * ATT_WIDTH
XBC_OFF = Z_OFF + SSM_WIDTH


def _params(*sem):
    return pltpu.CompilerParams(dimension_semantics=sem, vmem_limit_bytes=VMEM_LIMIT)


def _silu(x):
    h = 0.5 * x
    return h + h * jnp.tanh(h)


INPROJ_SUB = 512


def _inproj_kernel(x_ref, g_ref, w_ref, wdt_ref, qg_ref, kg_ref, o_ref, dt_ref, h_ref, *, tn):
    j = pl.program_id(1)
    assert tn >= K_OFF + ATT_WIDTH

    def columns(first_tile):
        h = h_ref[...]
        for lo in range(0, tn, INPROJ_SUB):
            hi = min(lo + INPROJ_SUB, tn)
            acc = jnp.dot(h, w_ref[:, lo:hi], preferred_element_type=F32)
            for c0 in range(lo, hi, ATT_HEAD_DIM):
                seg = acc[:, c0 - lo:c0 - lo + ATT_HEAD_DIM]
                if first_tile and c0 < K_OFF + ATT_WIDTH:
                    gain = qg_ref[...] if c0 < K_OFF else kg_ref[...]
                    ms = jnp.mean(seg * seg, axis=-1, keepdims=True)
                    seg = seg * lax.rsqrt(ms + EPS) * gain
                o_ref[:, c0:c0 + ATT_HEAD_DIM] = seg.astype(BF16)

    @pl.when(j == 0)
    def _():
        x = x_ref[...]
        ms = jnp.mean(x * x, axis=-1, keepdims=True)
        h = (x * lax.rsqrt(ms + EPS) * g_ref[...]).astype(BF16)
        h_ref[...] = h
        dt_ref[...] = jnp.dot(h, wdt_ref[...], preferred_element_type=F32)
        columns(True)

    @pl.when(j > 0)
    def _():
        columns(False)


def _in_proj(x2, g, w_main, w_dt, layer, qg, kg, *, tm, tn):
    t = x2.shape[0]
    grid = (t // tm, MAIN_DIM // tn)
    return pl.pallas_call(
        functools.partial(_inproj_kernel, tn=tn),
        grid=grid,
        in_specs=[
            pl.BlockSpec((tm, D_MODEL), lambda i, j: (i, 0)),
            pl.BlockSpec((1, D_MODEL), lambda i, j: (0, 0)),
            pl.BlockSpec((None, D_MODEL, tn), lambda i, j: (layer, 0, j)),
            pl.BlockSpec((None, D_MODEL, LANES), lambda i, j: (layer, 0, 0)),
            pl.BlockSpec((1, ATT_HEAD_DIM), lambda i, j: (0, 0)),
            pl.BlockSpec((1, ATT_HEAD_DIM), lambda i, j: (0, 0)),
        ],
        out_specs=[
            pl.BlockSpec((tm, tn), lambda i, j: (i, j)),
            pl.BlockSpec((tm, LANES), lambda i, j: (i, 0)),
        ],
        out_shape=[
            jax.ShapeDtypeStruct((t, MAIN_DIM), BF16),
            jax.ShapeDtypeStruct((t, LANES), F32),
        ],
        scratch_shapes=[pltpu.VMEM((tm, D_MODEL), BF16)],
        compiler_params=_params("arbitrary", "arbitrary"),
        name="in_proj",
    )(x2, g, w_main, w_dt, qg, kg)


ATT_TILE = 256
ATT_HEADS_PER_STEP = 8
LOG2_WEIGHT_FLOOR = -152.0


def _attn_kernel(q_ref, k_ref, v_ref, o_ref, r_ref, acc_ref):
    t = ATT_TILE
    heads = ATT_HEADS_PER_STEP
    rows = lax.broadcasted_iota(jnp.int32, (t, t), 0)
    cols = lax.broadcasted_iota(jnp.int32, (t, t), 1)
    later_sum = (rows > cols).astype(BF16)
    causal = cols < rows
    sign_bit = jnp.uint32(0x80000000)
    lanes = [slice(hh * ATT_HEAD_DIM, (hh + 1) * ATT_HEAD_DIM) for hh in range(heads)]

    def step(i, jblk, first):
        qs = pl.multiple_of(i * t, t)
        ks = pl.multiple_of(jblk * t, t)
        log_betas, log_remains, ws = [None] * heads, [None] * heads, [None] * heads
        alive = None
        for s in range(heads + 2):
            if s < heads:
                z = lax.dot_general(q_ref[pl.ds(qs, t), lanes[s]], k_ref[pl.ds(ks, t), lanes[s]],
                                    (((1,), (1,)), ((), ())), preferred_element_type=F32)
                neg_abs = lax.bitcast_convert_type(lax.bitcast_convert_type(z, jnp.uint32) | sign_bit, F32)
                log_beta = jnp.minimum(z, 0.0) - jnp.log2(1.0 + jnp.exp2(neg_abs))
                log_remain = log_beta - z
                if first:
                    log_remain = jnp.where(causal, log_remain, 0.0)
                log_betas[s], log_remains[s] = log_beta, log_remain
            if 0 <= s - 1 < heads:
                hh = s - 1
                within = jnp.dot(log_remains[hh].astype(BF16), later_sum, preferred_element_type=F32)
                w = jnp.exp2(log_betas[hh] + within)
                if first:
                    w = jnp.where(causal, w, 0.0)
                ws[hh] = w.astype(BF16)
            if 0 <= s - 2 < heads:
                hh = s - 2
                pv = jnp.dot(ws[hh], v_ref[pl.ds(ks, t), lanes[hh]], preferred_element_type=F32)
                block_sum = jnp.sum(log_remains[hh], axis=1, keepdims=True)
                if first:
                    acc_ref[hh] = pv
                    r_new = block_sum
                else:
                    r_old = r_ref[hh]
                    acc_ref[hh] += jnp.exp2(r_old) * pv
                    r_new = r_old + block_sum
                r_ref[hh] = r_new
                alive = r_new if alive is None else jnp.maximum(alive, r_new)
        return (jnp.max(alive) > LOG2_WEIGHT_FLOOR).astype(jnp.int32)

    def query_block(i, carry):
        go = step(i, i, True)

        def cond(c):
            n, go = c
            return jnp.logical_and(n < i, go > 0)

        def body(c):
            n, _ = c
            return n + 1, step(i, i - 1 - n, False)

        lax.while_loop(cond, body, (jnp.int32(0), go))
        for hh in range(heads):
            o_ref[pl.ds(pl.multiple_of(i * t, t), t), lanes[hh]] = acc_ref[hh].astype(o_ref.dtype)
        return carry

    lax.fori_loop(0, q_ref.shape[0] // t, query_block, 0)


def _attention(proj3):
    b, s, _ = proj3.shape
    width = ATT_HEADS_PER_STEP * ATT_HEAD_DIM
    grid = (b, ATT_HEADS // ATT_HEADS_PER_STEP)
    qb, kb, vb = Q_OFF // width, K_OFF // width, V_OFF // width
    return pl.pallas_call(
        _attn_kernel,
        grid=grid,
        in_specs=[
            pl.BlockSpec((None, s, width), lambda bi, h: (bi, 0, qb + h)),
            pl.BlockSpec((None, s, width), lambda bi, h: (bi, 0, kb + h)),
            pl.BlockSpec((None, s, width), lambda bi, h: (bi, 0, vb + h)),
        ],
        out_specs=pl.BlockSpec((None, s, width), lambda bi, h: (bi, 0, h)),
        out_shape=jax.ShapeDtypeStruct((b, s, ATT_WIDTH), BF16),
        scratch_shapes=[
            pltpu.VMEM((ATT_HEADS_PER_STEP, ATT_TILE, 1), F32),
            pltpu.VMEM((ATT_HEADS_PER_STEP, ATT_TILE, ATT_HEAD_DIM), F32),
        ],
        compiler_params=_params("arbitrary", "arbitrary"),
        name="attn",
    )(proj3, proj3, proj3)


def _split3(x):
    h1 = x.astype(BF16)
    r1 = x - h1.astype(F32)
    h2 = r1.astype(BF16)
    h3 = (r1 - h2.astype(F32)).astype(BF16)
    return h1, h2, h3


def _dot3(x, m, lhs=True):
    parts = _split3(x)
    if lhs:
        outs = [jnp.dot(p, m, preferred_element_type=F32) for p in parts]
    else:
        outs = [jnp.dot(m, p, preferred_element_type=F32) for p in parts]
    return outs[0] + outs[1] + outs[2]


SSD_CHUNKS_PER_STEP = 8


def _ssd_chunk(first, z, cur, prev, dt_raw, cw_ref, cb_ref, dtb_ref, alog_ref, dskip_ref, gain_ref,
               expand_ref, state_ref):
    L = SSD_CHUNK
    out_t = lax.broadcasted_iota(jnp.int32, ((CONV_WIDTH - 1) * L, 2 * L), 0)
    src_t = lax.broadcasted_iota(jnp.int32, ((CONV_WIDTH - 1) * L, 2 * L), 1)
    shift = out_t // L + 1
    hit = src_t == L + (out_t - (shift - 1) * L) - shift
    shifts = jnp.logical_and(hit, jnp.logical_or(src_t >= L, jnp.logical_not(first))).astype(BF16)
    pieces = []
    col = 0
    for cur_blk, prev_blk in zip(cur, prev):
        width = cur_blk.shape[1]
        both = jnp.concatenate([prev_blk, cur_blk], axis=0)
        back = jnp.dot(shifts, both, preferred_element_type=F32)
        acc = cb_ref[:, col:col + width] + cur_blk.astype(F32) * cw_ref[CONV_WIDTH - 1:CONV_WIDTH, col:col + width]
        for d in range(CONV_WIDTH - 1):
            tap = CONV_WIDTH - 2 - d
            acc = acc + back[d * L:(d + 1) * L] * cw_ref[tap:tap + 1, col:col + width]
        pieces.append(_silu(acc))
        col += width
    xs = jnp.concatenate(pieces[0:2], axis=1)
    b_in = pieces[2][:, 0:SSM_GROUPS * SSM_STATE]
    c_in = pieces[2][:, SSM_GROUPS * SSM_STATE:2 * SSM_GROUPS * SSM_STATE]

    lane = lax.broadcasted_iota(jnp.int32, (1, LANES), 1)
    a2 = jnp.where(lane < SSM_HEADS, -LOG2_E * jnp.exp(alog_ref[...]), 0.0)
    dt_pre = dt_raw + dtb_ref[...]
    dt = jnp.maximum(dt_pre, 0.0) + jnp.log(1.0 + jnp.exp(-jnp.abs(dt_pre)))
    da = dt * a2
    rows = lax.broadcasted_iota(jnp.int32, (L, L), 0)
    cols = lax.broadcasted_iota(jnp.int32, (L, L), 1)
    causal = rows >= cols
    not_causal = jnp.where(causal, 0.0, -jnp.inf)
    a_cum = _dot3(da, causal.astype(BF16), lhs=False)
    src_side = (a_cum - jnp.log2(dt)).T
    a_last = a_cum[L - 1:L, :]
    expand = expand_ref[...]
    decay_in = jnp.dot(jnp.exp2(a_cum).astype(BF16), expand, preferred_element_type=F32)
    w_end = jnp.dot((jnp.exp2(a_last - a_cum) * dt).astype(BF16), expand, preferred_element_type=F32)
    chunk_decay = _dot3(jnp.broadcast_to(jnp.exp2(a_last), (SUBLANES, LANES)), expand)[0:1, :]

    xw = (xs * w_end).astype(BF16)
    lane_in_pair = lax.broadcasted_iota(jnp.int32, (L, LANES), 1)
    y_parts = []
    for g in range(SSM_GROUPS):
        bg = b_in[:, g * SSM_STATE:(g + 1) * SSM_STATE]
        cg = c_in[:, g * SSM_STATE:(g + 1) * SSM_STATE].astype(BF16)
        cb = lax.dot_general(cg, bg.astype(BF16), (((1,), (1,)), ((), ())),
                             preferred_element_type=F32)
        gcols = slice(g * GROUP_WIDTH, (g + 1) * GROUP_WIDTH)
        st = state_ref[g]
        y_off = jnp.dot(cg, st.astype(BF16), preferred_element_type=F32) * decay_in[:, gcols]
        st_new = jnp.dot(bg.T.astype(BF16), xw[:, gcols], preferred_element_type=F32)
        state_ref[g] = st * chunk_decay[:, gcols] + st_new
        heads_per_group = SSM_HEADS // SSM_GROUPS
        for pair in range(heads_per_group // 2):
            h0 = g * heads_per_group + 2 * pair
            slab = xs[:, h0 * SSM_HEAD_DIM:(h0 + 2) * SSM_HEAD_DIM]
            mats = []
            for hh in (h0, h0 + 1):
                seg = (a_cum[:, hh:hh + 1] - src_side[hh:hh + 1, :]) + not_causal
                mats.append((cb * jnp.exp2(seg)).astype(BF16))
            lhs = jnp.concatenate(mats, axis=1)
            rhs = jnp.concatenate(
                [jnp.where(lane_in_pair < SSM_HEAD_DIM, slab, 0.0),
                 jnp.where(lane_in_pair >= SSM_HEAD_DIM, slab, 0.0)], axis=0).astype(BF16)
            y_diag = jnp.dot(lhs, rhs, preferred_element_type=F32)
            lo = (h0 - g * heads_per_group) * SSM_HEAD_DIM
            y_parts.append(y_diag + y_off[:, lo:lo + LANES])
    y = jnp.concatenate(y_parts, axis=1) + xs * dskip_ref[...]
    y = y * _silu(z.astype(F32))
    outs = []
    for g in range(SSM_GROUPS):
        yg = y[:, g * GROUP_WIDTH:(g + 1) * GROUP_WIDTH]
        ms = jnp.mean(yg * yg, axis=-1, keepdims=True)
        outs.append(yg * lax.rsqrt(ms + EPS))
    return (jnp.concatenate(outs, axis=1) * gain_ref[...]).astype(BF16)


def _ssd_kernel(z_ref, x0_ref, x1_ref, bc_ref, p0_ref, p1_ref, pbc_ref, dt_ref, cw_ref, cb_ref, dtb_ref,
                alog_ref, dskip_ref, gain_ref, expand_ref, *rest):
    n_weights = (len(rest) - 2) // 2
    w_refs, o_ref, w_outs, state_ref = (rest[:n_weights], rest[n_weights], rest[n_weights + 1:-1], rest[-1])
    c = pl.program_id(1)
    L = SSD_CHUNK

    for src, dst in zip(w_refs, w_outs):
        dst[...] = src[...].astype(BF16)

    @pl.when(c == 0)
    def _():
        state_ref[...] = jnp.zeros_like(state_ref)

    cur_refs = (x0_ref, x1_ref, bc_ref)
    prev = tuple(r[...] for r in (p0_ref, p1_ref, pbc_ref))
    first = c == 0
    for k in range(SSD_CHUNKS_PER_STEP):
        rows = slice(k * L, (k + 1) * L)
        cur = tuple(r[rows, :] for r in cur_refs)
        o_ref[rows, :] = _ssd_chunk(first, z_ref[rows, :], cur, prev, dt_ref[rows, :], cw_ref, cb_ref, dtb_ref,
                                    alog_ref, dskip_ref, gain_ref, expand_ref, state_ref)
        prev = cur
        first = jnp.bool_(False)


def _ssd(proj3, dt3, conv_w, conv_b, dt_bias, a_log, d_skip_x, gain, expand, weights):
    b, s, _ = proj3.shape
    L = SSD_CHUNK
    per_step = SSD_CHUNKS_PER_STEP if s % (SSD_CHUNKS_PER_STEP * L) == 0 else 1
    assert per_step == SSD_CHUNKS_PER_STEP
    rows = per_step * L
    half = SSM_WIDTH // 2
    grid = (b, s // rows)
    const = lambda shape: pl.BlockSpec(shape, lambda bi, c: (0, 0))
    xbc = lambda k: pl.BlockSpec((None, rows, half), lambda bi, c: (bi, c, XBC_OFF // half + k))
    xbc_prev = lambda k: pl.BlockSpec(
        (None, L, half), lambda bi, c: (bi, jnp.maximum(c * per_step - 1, 0), XBC_OFF // half + k))
    steps = grid[0] * grid[1]

    def slab(w, out):
        rows_total, cols = w.shape[1], w.shape[2]
        n = steps if rows_total % (steps * BF16_SUBLANES) == 0 else steps // 2
        assert rows_total % (n * BF16_SUBLANES) == 0
        per = steps // n
        return pl.BlockSpec((None, rows_total // n, cols),
                            lambda bi, c: (out, (bi * grid[1] + c) // per, 0))
    return pl.pallas_call(
        _ssd_kernel,
        grid=grid,
        in_specs=[
            pl.BlockSpec((None, rows, SSM_WIDTH), lambda bi, c: (bi, c, Z_OFF // SSM_WIDTH)),
            xbc(0), xbc(1), xbc(2),
            xbc_prev(0), xbc_prev(1), xbc_prev(2),
            pl.BlockSpec((None, rows, LANES), lambda bi, c: (bi, c, 0)),
            const((CONV_WIDTH, CONV_DIM)),
            const((1, CONV_DIM)),
            const((1, LANES)),
            const((1, LANES)),
            const((1, SSM_WIDTH)),
            const((1, SSM_WIDTH)),
            const((LANES, SSM_WIDTH)),
        ] + [slab(w, layer) for w, layer in weights],
        out_specs=[pl.BlockSpec((None, rows, SSM_WIDTH), lambda bi, c: (bi, c, 0))]
        + [slab(w, 0) for w, _ in weights],
        out_shape=[jax.ShapeDtypeStruct((b, s, SSM_WIDTH), BF16)]
        + [jax.ShapeDtypeStruct((1,) + w.shape[1:], BF16) for w, _ in weights],
        scratch_shapes=[pltpu.VMEM((SSM_GROUPS, SSM_STATE, GROUP_WIDTH), F32)],
        compiler_params=_params("arbitrary", "arbitrary"),
        name="ssd",
    )(proj3, proj3, proj3, proj3, proj3, proj3, proj3, dt3, conv_w, conv_b, dt_bias, a_log, d_skip_x,
      gain, expand, *[w for w, _ in weights])


OUTPROJ_ROW_GROUPS = 2


def _outproj_kernel(x_ref, oa_ref, os_ref, g_ref, w_ref, o_ref):
    rows = x_ref.shape[0] // OUTPROJ_ROW_GROUPS
    for r in range(OUTPROJ_ROW_GROUPS):
        sl = slice(r * rows, (r + 1) * rows)
        a = oa_ref[sl, :].astype(F32)
        ms = jnp.mean(a * a, axis=-1, keepdims=True)
        an = (a * lax.rsqrt(ms + EPS) * g_ref[...]).astype(BF16)
        y = jnp.dot(an, w_ref[0:ATT_WIDTH, :], preferred_element_type=F32)
        y = y + jnp.dot(os_ref[sl, :], w_ref[ATT_WIDTH:ATT_WIDTH + SSM_WIDTH, :], preferred_element_type=F32)
        o_ref[sl, :] = x_ref[sl, :] + y


def _out_proj(x2, o_att, o_ssm, gain, w_out, layer, *, tm):
    t = x2.shape[0]
    return pl.pallas_call(
        _outproj_kernel,
        grid=(t // tm,),
        in_specs=[
            pl.BlockSpec((tm, D_MODEL), lambda i: (i, 0)),
            pl.BlockSpec((tm, ATT_WIDTH), lambda i: (i, 0)),
            pl.BlockSpec((tm, SSM_WIDTH), lambda i: (i, 0)),
            pl.BlockSpec((1, ATT_WIDTH), lambda i: (0, 0)),
            pl.BlockSpec((None, ATT_WIDTH + SSM_WIDTH, D_MODEL), lambda i: (layer, 0, 0),
                         pipeline_mode=pl.Buffered(1)),
        ],
        out_specs=pl.BlockSpec((tm, D_MODEL), lambda i: (i, 0)),
        out_shape=jax.ShapeDtypeStruct((t, D_MODEL), F32),
        compiler_params=_params("arbitrary"),
        name="out_proj",
    )(x2, o_att, o_ssm, gain, w_out)


FFN_FIRST_STEP_ROW_GROUPS = 2


def _ffn_kernel(x_ref, g_ref, wg_ref, wu_ref, wd_ref, o_ref, h_ref):
    f = pl.program_id(1)

    def swiglu(h):
        gate = jnp.dot(h, wg_ref[...], preferred_element_type=F32)
        up = jnp.dot(h, wu_ref[...], preferred_element_type=F32)
        act = (_silu(gate) * up).astype(BF16)
        return jnp.dot(act, wd_ref[...], preferred_element_type=F32)

    @pl.when(f == 0)
    def _():
        rows = x_ref.shape[0] // FFN_FIRST_STEP_ROW_GROUPS
        for r in range(FFN_FIRST_STEP_ROW_GROUPS):
            sl = slice(r * rows, (r + 1) * rows)
            x = x_ref[sl, :]
            ms = jnp.mean(x * x, axis=-1, keepdims=True)
            h = (x * lax.rsqrt(ms + EPS) * g_ref[...]).astype(BF16)
            h_ref[sl, :] = h
            o_ref[sl, :] = x + swiglu(h)

    @pl.when(f > 0)
    def _():
        o_ref[...] += swiglu(h_ref[...])


def _ffn(x2, g, w_gate, w_up, w_down, layer, *, tm, tf):
    t = x2.shape[0]
    return pl.pallas_call(
        _ffn_kernel,
        grid=(t // tm, D_FF // tf),
        in_specs=[
            pl.BlockSpec((tm, D_MODEL), lambda i, f: (i, 0)),
            pl.BlockSpec((1, D_MODEL), lambda i, f: (0, 0)),
            pl.BlockSpec((None, D_MODEL, tf), lambda i, f: (layer, 0, f)),
            pl.BlockSpec((None, D_MODEL, tf), lambda i, f: (layer, 0, f)),
            pl.BlockSpec((None, tf, D_MODEL), lambda i, f: (layer, f, 0)),
        ],
        out_specs=pl.BlockSpec((tm, D_MODEL), lambda i, f: (i, 0)),
        out_shape=jax.ShapeDtypeStruct((t, D_MODEL), F32),
        scratch_shapes=[pltpu.VMEM((tm, D_MODEL), BF16)],
        compiler_params=_params("arbitrary", "arbitrary"),
        name="ffn",
    )(x2, g, w_gate, w_up, w_down)


def _tile(pref, n):
    return pref if n % pref == 0 else n


def _layer(x, layer, norm_mix, q_gain, k_gain, conv_w, conv_b, dt_bias, a_log, d_skip,
           attn_out_gain, ssm_out_gain, norm_ffn, w_main, w_dt, w_out, w_gate, w_up, w_down, expand):
    b, s, _ = x.shape
    t = b * s
    x2 = x.reshape(t, D_MODEL)
    row = lambda v: v.reshape(1, -1).astype(F32)
    pad_lanes = lambda v: jnp.pad(v.astype(F32), (0, LANES - v.shape[0])).reshape(1, LANES)

    proj, dt_raw = _in_proj(x2, row(norm_mix), w_main, w_dt, layer,
                            row(q_gain) * (LOG2_E * ATT_HEAD_DIM ** -0.5), row(k_gain),
                            tm=_tile(512, t), tn=MAIN_DIM // 2)
    proj3 = proj.reshape(b, s, MAIN_DIM)

    o_att = _attention(proj3)
    to_cast = [(w_out, layer), (w_gate, layer), (w_up, layer), (w_down, layer)]
    o_ssm, w_out, w_gate, w_up, w_down = _ssd(
        proj3, dt_raw.reshape(b, s, LANES), conv_w.astype(F32), row(conv_b), pad_lanes(dt_bias), pad_lanes(a_log),
        row(jnp.repeat(d_skip, SSM_HEAD_DIM)), row(ssm_out_gain), expand, to_cast)

    x2 = _out_proj(x2, o_att.reshape(t, ATT_WIDTH), o_ssm.reshape(t, SSM_WIDTH),
                   row(attn_out_gain), w_out, 0, tm=_tile(1024, t))
    x2 = _ffn(x2, row(norm_ffn), w_gate, w_up, w_down, 0, tm=_tile(1024, t), tf=512)
    return x2.reshape(b, s, D_MODEL)


def _inproj_weights(w_in):
    w_main = w_in.astype(BF16)
    w_dt = jnp.pad(w_in[:, :, MAIN_DIM:], ((0, 0), (0, 0), (0, LANES - SSM_HEADS))).astype(BF16)
    return w_main, w_dt


def _head_expand():
    head_of_lane = jnp.arange(SSM_WIDTH, dtype=jnp.int32) // SSM_HEAD_DIM
    return (jnp.arange(LANES, dtype=jnp.int32)[:, None] == head_of_lane[None, :]).astype(BF16)


def kernel(x, norm_mix, w_in, q_gain, k_gain, conv_w, conv_b, dt_bias, a_log, d_skip, attn_out_gain, ssm_out_gain, w_out, norm_ffn, w_gate, w_up, w_down):
    w_main, w_dt = _inproj_weights(w_in)
    expand = _head_expand()
    for i in range(norm_mix.shape[0]):
        x = _layer(x, i, norm_mix[i], q_gain[i], k_gain[i], conv_w[i], conv_b[i], dt_bias[i], a_log[i],
                   d_skip[i], attn_out_gain[i], ssm_out_gain[i], norm_ffn[i], w_main, w_dt,
                   w_out, w_gate, w_up, w_down, expand)
    return x
```

```python
import functools

import jax
import jax.numpy as jnp
from jax import lax
from jax.experimental import pallas as pl
from jax.experimental.pallas import tpu as pltpu

F32 = jnp.float32
BF16 = jnp.bfloat16

D_MODEL = 2048
ATT_WIDTH = 1024
SSM_WIDTH = 1024
ATT_HEAD_DIM = 128
ATT_HEADS = ATT_WIDTH // ATT_HEAD_DIM
SSM_HEAD_DIM = 64
SSM_HEADS = SSM_WIDTH // SSM_HEAD_DIM
SSM_GROUPS = 2
SSM_STATE = 128
GROUP_WIDTH = SSM_WIDTH // SSM_GROUPS
CONV_WIDTH = 4
CONV_DIM = SSM_WIDTH + 2 * SSM_GROUPS * SSM_STATE
SSD_CHUNK = 128
D_FF = 5632
EPS = 1e-6
LOG2_E = 1.4426950408889634

LANES = 128
SUBLANES = 8
BF16_SUBLANES = 16
VMEM_LIMIT = 56 * 1024 * 1024

MAIN_DIM = 3 * ATT_WIDTH + SSM_WIDTH + CONV_DIM
Q_OFF, K_OFF, V_OFF = 0, ATT_WIDTH, 2 * ATT_WIDTH
Z_OFF = 3 * ATT_WIDTH
XBC_OFF = Z_OFF + SSM_WIDTH


def _params(*sem):
    return pltpu.CompilerParams(dimension_semantics=sem, vmem_limit_bytes=VMEM_LIMIT)


def _silu(x):
    h = 0.5 * x
    return h + h * jnp.tanh(h)


INPROJ_SUB = 512


def _inproj_kernel(x_ref, g_ref, w_ref, wdt_ref, qg_ref, kg_ref, o_ref, dt_ref, h_ref, *, tn):
    j = pl.program_id(1)
    assert tn >= K_OFF + ATT_WIDTH

    def columns(first_tile):
        h = h_ref[...]
        for lo in range(0, tn, INPROJ_SUB):
            hi = min(lo + INPROJ_SUB, tn)
            acc = jnp.dot(h, w_ref[:, lo:hi], preferred_element_type=F32)
            for c0 in range(lo, hi, ATT_HEAD_DIM):
                seg = acc[:, c0 - lo:c0 - lo + ATT_HEAD_DIM]
                if first_tile and c0 < K_OFF + ATT_WIDTH:
                    gain = qg_ref[...] if c0 < K_OFF else kg_ref[...]
                    ms = jnp.mean(seg * seg, axis=-1, keepdims=True)
                    seg = seg * lax.rsqrt(ms + EPS) * gain
                o_ref[:, c0:c0 + ATT_HEAD_DIM] = seg.astype(BF16)

    @pl.when(j == 0)
    def _():
        x = x_ref[...]
        ms = jnp.mean(x * x, axis=-1, keepdims=True)
        h = (x * lax.rsqrt(ms + EPS) * g_ref[...]).astype(BF16)
        h_ref[...] = h
        dt_ref[...] = jnp.dot(h, wdt_ref[...], preferred_element_type=F32)
        columns(True)

    @pl.when(j > 0)
    def _():
        columns(False)


def _in_proj(x2, g, w_main, w_dt, layer, qg, kg, *, tm, tn):
    t = x2.shape[0]
    grid = (t // tm, MAIN_DIM // tn)
    return pl.pallas_call(
        functools.partial(_inproj_kernel, tn=tn),
        grid=grid,
        in_specs=[
            pl.BlockSpec((tm, D_MODEL), lambda i, j: (i, 0)),
            pl.BlockSpec((1, D_MODEL), lambda i, j: (0, 0)),
            pl.BlockSpec((None, D_MODEL, tn), lambda i, j: (layer, 0, j),
                         pipeline_mode=pl.Buffered(1) if tn == MAIN_DIM else None),
            pl.BlockSpec((None, D_MODEL, LANES), lambda i, j: (layer, 0, 0)),
            pl.BlockSpec((1, ATT_HEAD_DIM), lambda i, j: (0, 0)),
            pl.BlockSpec((1, ATT_HEAD_DIM), lambda i, j: (0, 0)),
        ],
        out_specs=[
            pl.BlockSpec((tm, tn), lambda i, j: (i, j)),
            pl.BlockSpec((tm, LANES), lambda i, j: (i, 0)),
        ],
        out_shape=[
            jax.ShapeDtypeStruct((t, MAIN_DIM), BF16),
            jax.ShapeDtypeStruct((t, LANES), F32),
        ],
        scratch_shapes=[pltpu.VMEM((tm, D_MODEL), BF16)],
        compiler_params=_params("arbitrary", "arbitrary"),
        name="in_proj",
    )(x2, g, w_main, w_dt, qg, kg)


ATT_TILE = 256
ATT_HEADS_PER_STEP = 8
LOG2_WEIGHT_FLOOR = -152.0


def _attn_kernel(q_ref, k_ref, v_ref, o_ref, r_ref, acc_ref):
    t = ATT_TILE
    heads = ATT_HEADS_PER_STEP
    rows = lax.broadcasted_iota(jnp.int32, (t, t), 0)
    cols = lax.broadcasted_iota(jnp.int32, (t, t), 1)
    later_sum = (rows > cols).astype(BF16)
    causal = cols < rows
    sign_bit = jnp.uint32(0x80000000)
    lanes = [slice(hh * ATT_HEAD_DIM, (hh + 1) * ATT_HEAD_DIM) for hh in range(heads)]

    def step(i, jblk, first):
        qs = pl.multiple_of(i * t, t)
        ks = pl.multiple_of(jblk * t, t)
        log_betas, log_remains, ws = [None] * heads, [None] * heads, [None] * heads
        alive = None
        for s in range(heads + 2):
            if s < heads:
                z = lax.dot_general(q_ref[pl.ds(qs, t), lanes[s]], k_ref[pl.ds(ks, t), lanes[s]],
                                    (((1,), (1,)), ((), ())), preferred_element_type=F32)
                neg_abs = lax.bitcast_convert_type(lax.bitcast_convert_type(z, jnp.uint32) | sign_bit, F32)
                log_beta = jnp.minimum(z, 0.0) - jnp.log2(1.0 + jnp.exp2(neg_abs))
                log_remain = log_beta - z
                if first:
                    log_remain = jnp.where(causal, log_remain, 0.0)
                log_betas[s], log_remains[s] = log_beta, log_remain
            if 0 <= s - 1 < heads:
                hh = s - 1
                within = jnp.dot(log_remains[hh].astype(BF16), later_sum, preferred_element_type=F32)
                w = jnp.exp2(log_betas[hh] + within)
                if first:
                    w = jnp.where(causal, w, 0.0)
                ws[hh] = w.astype(BF16)
            if 0 <= s - 2 < heads:
                hh = s - 2
                pv = jnp.dot(ws[hh], v_ref[pl.ds(ks, t), lanes[hh]], preferred_element_type=F32)
                block_sum = jnp.sum(log_remains[hh], axis=1, keepdims=True)
                if first:
                    acc_ref[hh] = pv
                    r_new = block_sum
                else:
                    r_old = r_ref[hh]
                    acc_ref[hh] += jnp.exp2(r_old) * pv
                    r_new = r_old + block_sum
                r_ref[hh] = r_new
                alive = r_new if alive is None else jnp.maximum(alive, r_new)
        return (jnp.max(alive) > LOG2_WEIGHT_FLOOR).astype(jnp.int32)

    def query_block(i, carry):
        go = step(i, i, True)

        def cond(c):
            n, go = c
            return jnp.logical_and(n < i, go > 0)

        def body(c):
            n, _ = c
            return n + 1, step(i, i - 1 - n, False)

        lax.while_loop(cond, body, (jnp.int32(0), go))
        for hh in range(heads):
            o_ref[pl.ds(pl.multiple_of(i * t, t), t), lanes[hh]] = acc_ref[hh].astype(o_ref.dtype)
        return carry

    lax.fori_loop(0, q_ref.shape[0] // t, query_block, 0)


def _attention(proj3):
    b, s, _ = proj3.shape
    width = ATT_HEADS_PER_STEP * ATT_HEAD_DIM
    grid = (b, ATT_HEADS // ATT_HEADS_PER_STEP)
    qb, kb, vb = Q_OFF // width, K_OFF // width, V_OFF // width
    return pl.pallas_call(
        _attn_kernel,
        grid=grid,
        in_specs=[
            pl.BlockSpec((None, s, width), lambda bi, h: (bi, 0, qb + h)),
            pl.BlockSpec((None, s, width), lambda bi, h: (bi, 0, kb + h)),
            pl.BlockSpec((None, s, width), lambda bi, h: (bi, 0, vb + h)),
        ],
        out_specs=pl.BlockSpec((None, s, width), lambda bi, h: (bi, 0, h)),
        out_shape=jax.ShapeDtypeStruct((b, s, ATT_WIDTH), BF16),
        scratch_shapes=[
            pltpu.VMEM((ATT_HEADS_PER_STEP, ATT_TILE, 1), F32),
            pltpu.VMEM((ATT_HEADS_PER_STEP, ATT_TILE, ATT_HEAD_DIM), F32),
        ],
        compiler_params=_params("arbitrary", "arbitrary"),
        name="attn",
    )(proj3, proj3, proj3)


def _split3(x):
    h1 = x.astype(BF16)
    r1 = x - h1.astype(F32)
    h2 = r1.astype(BF16)
    h3 = (r1 - h2.astype(F32)).astype(BF16)
    return h1, h2, h3


def _dot3(x, m, lhs=True):
    parts = _split3(x)
    if lhs:
        outs = [jnp.dot(p, m, preferred_element_type=F32) for p in parts]
    else:
        outs = [jnp.dot(m, p, preferred_element_type=F32) for p in parts]
    return outs[0] + outs[1] + outs[2]


SSD_CHUNKS_PER_STEP = 8


def _ssd_chunk(first, z, cur, prev, dt_raw, cw_ref, cb_ref, dtb_ref, alog_ref, dskip_ref, gain_ref,
               expand_ref, state_ref):
    L = SSD_CHUNK
    out_t = lax.broadcasted_iota(jnp.int32, ((CONV_WIDTH - 1) * L, 2 * L), 0)
    src_t = lax.broadcasted_iota(jnp.int32, ((CONV_WIDTH - 1) * L, 2 * L), 1)
    shift = out_t // L + 1
    hit = src_t == L + (out_t - (shift - 1) * L) - shift
    shifts = jnp.logical_and(hit, jnp.logical_or(src_t >= L, jnp.logical_not(first))).astype(BF16)
    pieces = []
    col = 0
    for cur_blk, prev_blk in zip(cur, prev):
        width = cur_blk.shape[1]
        both = jnp.concatenate([prev_blk, cur_blk], axis=0)
        back = jnp.dot(shifts, both, preferred_element_type=F32)
        acc = cb_ref[:, col:col + width] + cur_blk.astype(F32) * cw_ref[CONV_WIDTH - 1:CONV_WIDTH, col:col + width]
        for d in range(CONV_WIDTH - 1):
            tap = CONV_WIDTH - 2 - d
            acc = acc + back[d * L:(d + 1) * L] * cw_ref[tap:tap + 1, col:col + width]
        pieces.append(_silu(acc))
        col += width
    xs = jnp.concatenate(pieces[0:2], axis=1)
    b_in = pieces[2][:, 0:SSM_GROUPS * SSM_STATE]
    c_in = pieces[2][:, SSM_GROUPS * SSM_STATE:2 * SSM_GROUPS * SSM_STATE]

    lane = lax.broadcasted_iota(jnp.int32, (1, LANES), 1)
    a2 = jnp.where(lane < SSM_HEADS, -LOG2_E * jnp.exp(alog_ref[...]), 0.0)
    dt_pre = dt_raw + dtb_ref[...]
    dt = jnp.maximum(dt_pre, 0.0) + jnp.log(1.0 + jnp.exp(-jnp.abs(dt_pre)))
    da = dt * a2
    rows = lax.broadcasted_iota(jnp.int32, (L, L), 0)
    cols = lax.broadcasted_iota(jnp.int32, (L, L), 1)
    causal = rows >= cols
    not_causal = jnp.where(causal, 0.0, -jnp.inf)
    a_cum = _dot3(da, causal.astype(BF16), lhs=False)
    src_side = (a_cum - jnp.log2(dt)).T
    a_last = a_cum[L - 1:L, :]
    expand = expand_ref[...]
    decay_in = jnp.dot(jnp.exp2(a_cum).astype(BF16), expand, preferred_element_type=F32)
    w_end = jnp.dot((jnp.exp2(a_last - a_cum) * dt).astype(BF16), expand, preferred_element_type=F32)
    chunk_decay = _dot3(jnp.broadcast_to(jnp.exp2(a_last), (SUBLANES, LANES)), expand)[0:1, :]

    xw = (xs * w_end).astype(BF16)
    lane_in_pair = lax.broadcasted_iota(jnp.int32, (L, LANES), 1)
    y_parts = []
    for g in range(SSM_GROUPS):
        bg = b_in[:, g * SSM_STATE:(g + 1) * SSM_STATE]
        cg = c_in[:, g * SSM_STATE:(g + 1) * SSM_STATE].astype(BF16)
        cb = lax.dot_general(cg, bg.astype(BF16), (((1,), (1,)), ((), ())),
                             preferred_element_type=F32)
        gcols = slice(g * GROUP_WIDTH, (g + 1) * GROUP_WIDTH)
        st = state_ref[g]
        y_off = jnp.dot(cg, st.astype(BF16), preferred_element_type=F32) * decay_in[:, gcols]
        st_new = jnp.dot(bg.T.astype(BF16), xw[:, gcols], preferred_element_type=F32)
        state_ref[g] = st * chunk_decay[:, gcols] + st_new
        heads_per_group = SSM_HEADS // SSM_GROUPS
        for pair in range(heads_per_group // 2):
            h0 = g * heads_per_group + 2 * pair
            slab = xs[:, h0 * SSM_HEAD_DIM:(h0 + 2) * SSM_HEAD_DIM]
            mats = []
            for hh in (h0, h0 + 1):
                seg = (a_cum[:, hh:hh + 1] - src_side[hh:hh + 1, :]) + not_causal
                mats.append((cb * jnp.exp2(seg)).astype(BF16))
            lhs = jnp.concatenate(mats, axis=1)
            rhs = jnp.concatenate(
                [jnp.where(lane_in_pair < SSM_HEAD_DIM, slab, 0.0),
                 jnp.where(lane_in_pair >= SSM_HEAD_DIM, slab, 0.0)], axis=0).astype(BF16)
            y_diag = jnp.dot(lhs, rhs, preferred_element_type=F32)
            lo = (h0 - g * heads_per_group) * SSM_HEAD_DIM
            y_parts.append(y_diag + y_off[:, lo:lo + LANES])
    y = jnp.concatenate(y_parts, axis=1) + xs * dskip_ref[...]
    y = y * _silu(z.astype(F32))
    outs = []
    for g in range(SSM_GROUPS):
        yg = y[:, g * GROUP_WIDTH:(g + 1) * GROUP_WIDTH]
        ms = jnp.mean(yg * yg, axis=-1, keepdims=True)
        outs.append(yg * lax.rsqrt(ms + EPS))
    return (jnp.concatenate(outs, axis=1) * gain_ref[...]).astype(BF16)


def _ssd_kernel(z_ref, x0_ref, x1_ref, bc_ref, p0_ref, p1_ref, pbc_ref, dt_ref, cw_ref, cb_ref, dtb_ref,
                alog_ref, dskip_ref, gain_ref, expand_ref, *rest):
    n_weights = (len(rest) - 2) // 2
    w_refs, o_ref, w_outs, state_ref = (rest[:n_weights], rest[n_weights], rest[n_weights + 1:-1], rest[-1])
    c = pl.program_id(1)
    L = SSD_CHUNK

    for src, dst in zip(w_refs, w_outs):
        dst[...] = src[...].astype(BF16)

    @pl.when(c == 0)
    def _():
        state_ref[...] = jnp.zeros_like(state_ref)

    cur_refs = (x0_ref, x1_ref, bc_ref)
    prev = tuple(r[...] for r in (p0_ref, p1_ref, pbc_ref))
    first = c == 0
    for k in range(SSD_CHUNKS_PER_STEP):
        rows = slice(k * L, (k + 1) * L)
        cur = tuple(r[rows, :] for r in cur_refs)
        o_ref[rows, :] = _ssd_chunk(first, z_ref[rows, :], cur, prev, dt_ref[rows, :], cw_ref, cb_ref, dtb_ref,
                                    alog_ref, dskip_ref, gain_ref, expand_ref, state_ref)
        prev = cur
        first = jnp.bool_(False)


def _ssd(proj3, dt3, conv_w, conv_b, dt_bias, a_log, d_skip_x, gain, expand, weights):
    b, s, _ = proj3.shape
    L = SSD_CHUNK
    per_step = SSD_CHUNKS_PER_STEP if s % (SSD_CHUNKS_PER_STEP * L) == 0 else 1
    assert per_step == SSD_CHUNKS_PER_STEP
    rows = per_step * L
    half = SSM_WIDTH // 2
    grid = (b, s // rows)
    const = lambda shape: pl.BlockSpec(shape, lambda bi, c: (0, 0))
    xbc = lambda k: pl.BlockSpec((None, rows, half), lambda bi, c: (bi, c, XBC_OFF // half + k))
    xbc_prev = lambda k: pl.BlockSpec(
        (None, L, half), lambda bi, c: (bi, jnp.maximum(c * per_step - 1, 0), XBC_OFF // half + k))
    steps = grid[0] * grid[1]

    def slab(w, out):
        rows_total, cols = w.shape[1], w.shape[2]
        n = steps if rows_total % (steps * BF16_SUBLANES) == 0 else steps // 2
        assert rows_total % (n * BF16_SUBLANES) == 0
        per = steps // n
        return pl.BlockSpec((None, rows_total // n, cols),
                            lambda bi, c: (out, (bi * grid[1] + c) // per, 0))
    return pl.pallas_call(
        _ssd_kernel,
        grid=grid,
        in_specs=[
            pl.BlockSpec((None, rows, SSM_WIDTH), lambda bi, c: (bi, c, Z_OFF // SSM_WIDTH)),
            xbc(0), xbc(1), xbc(2),
            xbc_prev(0), xbc_prev(1), xbc_prev(2),
            pl.BlockSpec((None, rows, LANES), lambda bi, c: (bi, c, 0)),
            const((CONV_WIDTH, CONV_DIM)),
            const((1, CONV_DIM)),
            const((1, LANES)),
            const((1, LANES)),
            const((1, SSM_WIDTH)),
            const((1, SSM_WIDTH)),
            const((LANES, SSM_WIDTH)),
        ] + [slab(w, layer) for w, layer in weights],
        out_specs=[pl.BlockSpec((None, rows, SSM_WIDTH), lambda bi, c: (bi, c, 0))]
        + [slab(w, 0) for w, _ in weights],
        out_shape=[jax.ShapeDtypeStruct((b, s, SSM_WIDTH), BF16)]
        + [jax.ShapeDtypeStruct((1,) + w.shape[1:], BF16) for w, _ in weights],
        scratch_shapes=[pltpu.VMEM((SSM_GROUPS, SSM_STATE, GROUP_WIDTH), F32)],
        compiler_params=_params("arbitrary", "arbitrary"),
        name="ssd",
    )(proj3, proj3, proj3, proj3, proj3, proj3, proj3, dt3, conv_w, conv_b, dt_bias, a_log, d_skip_x,
      gain, expand, *[w for w, _ in weights])


OUTPROJ_ROW_GROUPS = 2


def _outproj_kernel(x_ref, oa_ref, os_ref, g_ref, w_ref, o_ref):
    rows = x_ref.shape[0] // OUTPROJ_ROW_GROUPS
    for r in range(OUTPROJ_ROW_GROUPS):
        sl = slice(r * rows, (r + 1) * rows)
        a = oa_ref[sl, :].astype(F32)
        ms = jnp.mean(a * a, axis=-1, keepdims=True)
        an = (a * lax.rsqrt(ms + EPS) * g_ref[...]).astype(BF16)
        y = jnp.dot(an, w_ref[0:ATT_WIDTH, :], preferred_element_type=F32)
        y = y + jnp.dot(os_ref[sl, :], w_ref[ATT_WIDTH:ATT_WIDTH + SSM_WIDTH, :], preferred_element_type=F32)
        o_ref[sl, :] = x_ref[sl, :] + y


def _out_proj(x2, o_att, o_ssm, gain, w_out, layer, *, tm):
    t = x2.shape[0]
    return pl.pallas_call(
        _outproj_kernel,
        grid=(t // tm,),
        in_specs=[
            pl.BlockSpec((tm, D_MODEL), lambda i: (i, 0)),
            pl.BlockSpec((tm, ATT_WIDTH), lambda i: (i, 0)),
            pl.BlockSpec((tm, SSM_WIDTH), lambda i: (i, 0)),
            pl.BlockSpec((1, ATT_WIDTH), lambda i: (0, 0)),
            pl.BlockSpec((None, ATT_WIDTH + SSM_WIDTH, D_MODEL), lambda i: (layer, 0, 0),
                         pipeline_mode=pl.Buffered(1)),
        ],
        out_specs=pl.BlockSpec((tm, D_MODEL), lambda i: (i, 0)),
        out_shape=jax.ShapeDtypeStruct((t, D_MODEL), F32),
        compiler_params=_params("arbitrary"),
        name="out_proj",
    )(x2, o_att, o_ssm, gain, w_out)


FFN_FIRST_STEP_ROW_GROUPS = 2


def _ffn_kernel(x_ref, g_ref, wg_ref, wu_ref, wd_ref, o_ref, h_ref):
    f = pl.program_id(1)

    def swiglu(h):
        gate = jnp.dot(h, wg_ref[...], preferred_element_type=F32)
        up = jnp.dot(h, wu_ref[...], preferred_element_type=F32)
        act = (_silu(gate) * up).astype(BF16)
        return jnp.dot(act, wd_ref[...], preferred_element_type=F32)

    @pl.when(f == 0)
    def _():
        rows = x_ref.shape[0] // FFN_FIRST_STEP_ROW_GROUPS
        for r in range(FFN_FIRST_STEP_ROW_GROUPS):
            sl = slice(r * rows, (r + 1) * rows)
            x = x_ref[sl, :]
            ms = jnp.mean(x * x, axis=-1, keepdims=True)
            h = (x * lax.rsqrt(ms + EPS) * g_ref[...]).astype(BF16)
            h_ref[sl, :] = h
            o_ref[sl, :] = x + swiglu(h)

    @pl.when(f > 0)
    def _():
        o_ref[...] += swiglu(h_ref[...])


def _ffn(x2, g, w_gate, w_up, w_down, layer, *, tm, tf):
    t = x2.shape[0]
    return pl.pallas_call(
        _ffn_kernel,
        grid=(t // tm, D_FF // tf),
        in_specs=[
            pl.BlockSpec((tm, D_MODEL), lambda i, f: (i, 0)),
            pl.BlockSpec((1, D_MODEL), lambda i, f: (0, 0)),
            pl.BlockSpec((None, D_MODEL, tf), lambda i, f: (layer, 0, f)),
            pl.BlockSpec((None, D_MODEL, tf), lambda i, f: (layer, 0, f)),
            pl.BlockSpec((None, tf, D_MODEL), lambda i, f: (layer, f, 0)),
        ],
        out_specs=pl.BlockSpec((tm, D_MODEL), lambda i, f: (i, 0)),
        out_shape=jax.ShapeDtypeStruct((t, D_MODEL), F32),
        scratch_shapes=[pltpu.VMEM((tm, D_MODEL), BF16)],
        compiler_params=_params("arbitrary", "arbitrary"),
        name="ffn",
    )(x2, g, w_gate, w_up, w_down)


def _tile(pref, n):
    return pref if n % pref == 0 else n


def _layer(x, layer, norm_mix, q_gain, k_gain, conv_w, conv_b, dt_bias, a_log, d_skip,
           attn_out_gain, ssm_out_gain, norm_ffn, w_main, w_dt, w_out, w_gate, w_up, w_down, expand):
    b, s, _ = x.shape
    t = b * s
    x2 = x.reshape(t, D_MODEL)
    row = lambda v: v.reshape(1, -1).astype(F32)
    pad_lanes = lambda v: jnp.pad(v.astype(F32), (0, LANES - v.shape[0])).reshape(1, LANES)

    proj, dt_raw = _in_proj(x2, row(norm_mix), w_main, w_dt, layer,
                            row(q_gain) * (LOG2_E * ATT_HEAD_DIM ** -0.5), row(k_gain),
                            tm=_tile(512, t), tn=MAIN_DIM)
    proj3 = proj.reshape(b, s, MAIN_DIM)

    o_att = _attention(proj3)
    to_cast = [(w_out, layer), (w_gate, layer), (w_up, layer), (w_down, layer)]
    o_ssm, w_out, w_gate, w_up, w_down = _ssd(
        proj3, dt_raw.reshape(b, s, LANES), conv_w.astype(F32), row(conv_b), pad_lanes(dt_bias), pad_lanes(a_log),
        row(jnp.repeat(d_skip, SSM_HEAD_DIM)), row(ssm_out_gain), expand, to_cast)

    x2 = _out_proj(x2, o_att.reshape(t, ATT_WIDTH), o_ssm.reshape(t, SSM_WIDTH),
                   row(attn_out_gain), w_out, 0, tm=_tile(1024, t))
    x2 = _ffn(x2, row(norm_ffn), w_gate, w_up, w_down, 0, tm=_tile(1024, t), tf=512)
    return x2.reshape(b, s, D_MODEL)


def _inproj_weights(w_in):
    w_main = w_in.astype(BF16)
    w_dt = jnp.pad(w_in[:, :, MAIN_DIM:], ((0, 0), (0, 0), (0, LANES - SSM_HEADS))).astype(BF16)
    return w_main, w_dt


def _head_expand():
    head_of_lane = jnp.arange(SSM_WIDTH, dtype=jnp.int32) // SSM_HEAD_DIM
    return (jnp.arange(LANES, dtype=jnp.int32)[:, None] == head_of_lane[None, :]).astype(BF16)


def kernel(x, norm_mix, w_in, q_gain, k_gain, conv_w, conv_b, dt_bias, a_log, d_skip, attn_out_gain, ssm_out_gain, w_out, norm_ffn, w_gate, w_up, w_down):
    w_main, w_dt = _inproj_weights(w_in)
    expand = _head_expand()
    for i in range(norm_mix.shape[0]):
        x = _layer(x, i, norm_mix[i], q_gain[i], k_gain[i], conv_w[i], conv_b[i], dt_bias[i], a_log[i],
                   d_skip[i], attn_out_gain[i], ssm_out_gain[i], norm_ffn[i], w_main, w_dt,
                   w_out, w_gate, w_up, w_down, expand)
    return x
```

```python
import functools

import jax
import jax.numpy as jnp
from jax import lax
from jax.experimental import pallas as pl
from jax.experimental.pallas import tpu as pltpu

F32 = jnp.float32
BF16 = jnp.bfloat16

D_MODEL = 2048
ATT_WIDTH = 1024
SSM_WIDTH = 1024
ATT_HEAD_DIM = 128
ATT_HEADS = ATT_WIDTH // ATT_HEAD_DIM
SSM_HEAD_DIM = 64
SSM_HEADS = SSM_WIDTH // SSM_HEAD_DIM
SSM_GROUPS = 2
SSM_STATE = 128
GROUP_WIDTH = SSM_WIDTH // SSM_GROUPS
CONV_WIDTH = 4
CONV_DIM = SSM_WIDTH + 2 * SSM_GROUPS * SSM_STATE
SSD_CHUNK = 128
D_FF = 5632
EPS = 1e-6
LOG2_E = 1.4426950408889634

LANES = 128
SUBLANES = 8
BF16_SUBLANES = 16
VMEM_LIMIT = 56 * 1024 * 1024

MAIN_DIM = 3 * ATT_WIDTH + SSM_WIDTH + CONV_DIM
Q_OFF, K_OFF, V_OFF = 0, ATT_WIDTH, 2 * ATT_WIDTH
Z_OFF = 3 * ATT_WIDTH
XBC_OFF = Z_OFF + SSM_WIDTH


def _params(*sem):
    return pltpu.CompilerParams(dimension_semantics=sem, vmem_limit_bytes=VMEM_LIMIT)


def _silu(x):
    h = 0.5 * x
    return h + h * jnp.tanh(h)


INPROJ_SUB = 512


def _inproj_kernel(x_ref, g_ref, w_ref, wdt_ref, qg_ref, kg_ref, o_ref, dt_ref, h_ref, *, tn):
    j = pl.program_id(1)
    assert tn >= K_OFF + ATT_WIDTH

    def columns(first_tile):
        h = h_ref[...]
        for lo in range(0, tn, INPROJ_SUB):
            hi = min(lo + INPROJ_SUB, tn)
            acc = jnp.dot(h, w_ref[:, lo:hi], preferred_element_type=F32)
            for c0 in range(lo, hi, ATT_HEAD_DIM):
                seg = acc[:, c0 - lo:c0 - lo + ATT_HEAD_DIM]
                if first_tile and c0 < K_OFF + ATT_WIDTH:
                    gain = qg_ref[...] if c0 < K_OFF else kg_ref[...]
                    ms = jnp.mean(seg * seg, axis=-1, keepdims=True)
                    seg = seg * lax.rsqrt(ms + EPS) * gain
                o_ref[:, c0:c0 + ATT_HEAD_DIM] = seg.astype(BF16)

    @pl.when(j == 0)
    def _():
        x = x_ref[...]
        ms = jnp.mean(x * x, axis=-1, keepdims=True)
        h = (x * lax.rsqrt(ms + EPS) * g_ref[...]).astype(BF16)
        h_ref[...] = h
        dt_ref[...] = jnp.dot(h, wdt_ref[...], preferred_element_type=F32)
        columns(True)

    @pl.when(j > 0)
    def _():
        columns(False)


def _in_proj(x2, g, w_main, w_dt, layer, qg, kg, *, tm, tn):
    t = x2.shape[0]
    grid = (t // tm, MAIN_DIM // tn)
    return pl.pallas_call(
        functools.partial(_inproj_kernel, tn=tn),
        grid=grid,
        in_specs=[
            pl.BlockSpec((tm, D_MODEL), lambda i, j: (i, 0)),
            pl.BlockSpec((1, D_MODEL), lambda i, j: (0, 0)),
            pl.BlockSpec((None, D_MODEL, tn), lambda i, j: (layer, 0, j),
                         pipeline_mode=pl.Buffered(1) if tn == MAIN_DIM else None),
            pl.BlockSpec((None, D_MODEL, LANES), lambda i, j: (layer, 0, 0)),
            pl.BlockSpec((1, ATT_HEAD_DIM), lambda i, j: (0, 0)),
            pl.BlockSpec((1, ATT_HEAD_DIM), lambda i, j: (0, 0)),
        ],
        out_specs=[
            pl.BlockSpec((tm, tn), lambda i, j: (i, j)),
            pl.BlockSpec((tm, LANES), lambda i, j: (i, 0)),
        ],
        out_shape=[
            jax.ShapeDtypeStruct((t, MAIN_DIM), BF16),
            jax.ShapeDtypeStruct((t, LANES), F32),
        ],
        scratch_shapes=[pltpu.VMEM((tm, D_MODEL), BF16)],
        compiler_params=_params("arbitrary", "arbitrary"),
        name="in_proj",
    )(x2, g, w_main, w_dt, qg, kg)


ATT_TILE = 256
ATT_HEADS_PER_STEP = 8
LOG2_WEIGHT_FLOOR = -152.0


def _attn_kernel(q_ref, k_ref, v_ref, o_ref, r_ref, acc_ref):
    t = ATT_TILE
    heads = ATT_HEADS_PER_STEP
    rows = lax.broadcasted_iota(jnp.int32, (t, t), 0)
    cols = lax.broadcasted_iota(jnp.int32, (t, t), 1)
    later_sum = (rows > cols).astype(BF16)
    causal = cols < rows
    sign_bit = jnp.uint32(0x80000000)
    lanes = [slice(hh * ATT_HEAD_DIM, (hh + 1) * ATT_HEAD_DIM) for hh in range(heads)]

    def step(i, blocks, start):
        qs = pl.multiple_of(i * t, t)
        starts = [pl.multiple_of(jblk * t, t) for jblk, _ in blocks]
        log_betas, log_remains, ws = [None] * heads, [None] * heads, [None] * heads
        alive = None
        for s in range(heads + 2):
            if s < heads:
                log_betas[s], log_remains[s] = [], []
                for ks, (_, diagonal) in zip(starts, blocks):
                    z = lax.dot_general(q_ref[pl.ds(qs, t), lanes[s]], k_ref[pl.ds(ks, t), lanes[s]],
                                        (((1,), (1,)), ((), ())), preferred_element_type=F32)
                    neg_abs = lax.bitcast_convert_type(lax.bitcast_convert_type(z, jnp.uint32) | sign_bit, F32)
                    log_beta = jnp.minimum(z, 0.0) - jnp.log2(1.0 + jnp.exp2(neg_abs))
                    log_remain = log_beta - z
                    if diagonal:
                        log_remain = jnp.where(causal, log_remain, 0.0)
                    log_betas[s].append(log_beta)
                    log_remains[s].append(log_remain)
            if 0 <= s - 1 < heads:
                hh = s - 1
                ws[hh] = []
                for p, (_, diagonal) in enumerate(blocks):
                    within = jnp.dot(log_remains[hh][p].astype(BF16), later_sum, preferred_element_type=F32)
                    w = jnp.exp2(log_betas[hh][p] + within)
                    if diagonal:
                        w = jnp.where(causal, w, 0.0)
                    ws[hh].append(w.astype(BF16))
            if 0 <= s - 2 < heads:
                hh = s - 2
                r, acc = (None, None) if start else (r_ref[hh], acc_ref[hh])
                for p, ks in enumerate(starts):
                    pv = jnp.dot(ws[hh][p], v_ref[pl.ds(ks, t), lanes[hh]], preferred_element_type=F32)
                    block_sum = jnp.sum(log_remains[hh][p], axis=1, keepdims=True)
                    if r is None:
                        r, acc = block_sum, pv
                    else:
                        r, acc = r + block_sum, acc + jnp.exp2(r) * pv
                r_ref[hh] = r
                acc_ref[hh] = acc
                alive = r if alive is None else jnp.maximum(alive, r)
        return (jnp.max(alive) > LOG2_WEIGHT_FLOOR).astype(jnp.int32)

    def finish(i):
        for hh in range(heads):
            o_ref[pl.ds(pl.multiple_of(i * t, t), t), lanes[hh]] = acc_ref[hh].astype(o_ref.dtype)

    step(0, ((0, True),), True)
    finish(0)

    def query_block(i, carry):
        go = step(i, ((i, True), (i - 1, False)), True)

        def cond(c):
            n, go = c
            return jnp.logical_and(n < i, go > 0)

        def body(c):
            n, _ = c
            return n + 1, step(i, ((i - 1 - n, False),), False)

        lax.while_loop(cond, body, (jnp.int32(1), go))
        finish(i)
        return carry

    lax.fori_loop(1, q_ref.shape[0] // t, query_block, 0)


def _attention(proj3):
    b, s, _ = proj3.shape
    width = ATT_HEADS_PER_STEP * ATT_HEAD_DIM
    grid = (b, ATT_HEADS // ATT_HEADS_PER_STEP)
    qb, kb, vb = Q_OFF // width, K_OFF // width, V_OFF // width
    return pl.pallas_call(
        _attn_kernel,
        grid=grid,
        in_specs=[
            pl.BlockSpec((None, s, width), lambda bi, h: (bi, 0, qb + h)),
            pl.BlockSpec((None, s, width), lambda bi, h: (bi, 0, kb + h)),
            pl.BlockSpec((None, s, width), lambda bi, h: (bi, 0, vb + h)),
        ],
        out_specs=pl.BlockSpec((None, s, width), lambda bi, h: (bi, 0, h)),
        out_shape=jax.ShapeDtypeStruct((b, s, ATT_WIDTH), BF16),
        scratch_shapes=[
            pltpu.VMEM((ATT_HEADS_PER_STEP, ATT_TILE, 1), F32),
            pltpu.VMEM((ATT_HEADS_PER_STEP, ATT_TILE, ATT_HEAD_DIM), F32),
        ],
        compiler_params=_params("arbitrary", "arbitrary"),
        name="attn",
    )(proj3, proj3, proj3)


def _split3(x):
    h1 = x.astype(BF16)
    r1 = x - h1.astype(F32)
    h2 = r1.astype(BF16)
    h3 = (r1 - h2.astype(F32)).astype(BF16)
    return h1, h2, h3


def _dot3(x, m, lhs=True):
    parts = _split3(x)
    if lhs:
        outs = [jnp.dot(p, m, preferred_element_type=F32) for p in parts]
    else:
        outs = [jnp.dot(m, p, preferred_element_type=F32) for p in parts]
    return outs[0] + outs[1] + outs[2]


SSD_CHUNKS_PER_STEP = 8


def _ssd_chunk(first, z, cur, prev, dt_raw, cw_ref, cb_ref, dtb_ref, alog_ref, dskip_ref, gain_ref,
               expand_ref, state_ref):
    L = SSD_CHUNK
    out_t = lax.broadcasted_iota(jnp.int32, ((CONV_WIDTH - 1) * L, 2 * L), 0)
    src_t = lax.broadcasted_iota(jnp.int32, ((CONV_WIDTH - 1) * L, 2 * L), 1)
    shift = out_t // L + 1
    hit = src_t == L + (out_t - (shift - 1) * L) - shift
    shifts = jnp.logical_and(hit, jnp.logical_or(src_t >= L, jnp.logical_not(first))).astype(BF16)
    pieces = []
    col = 0
    for cur_blk, prev_blk in zip(cur, prev):
        width = cur_blk.shape[1]
        both = jnp.concatenate([prev_blk, cur_blk], axis=0)
        back = jnp.dot(shifts, both, preferred_element_type=F32)
        acc = cb_ref[:, col:col + width] + cur_blk.astype(F32) * cw_ref[CONV_WIDTH - 1:CONV_WIDTH, col:col + width]
        for d in range(CONV_WIDTH - 1):
            tap = CONV_WIDTH - 2 - d
            acc = acc + back[d * L:(d + 1) * L] * cw_ref[tap:tap + 1, col:col + width]
        pieces.append(_silu(acc))
        col += width
    xs = jnp.concatenate(pieces[0:2], axis=1)
    b_in = pieces[2][:, 0:SSM_GROUPS * SSM_STATE]
    c_in = pieces[2][:, SSM_GROUPS * SSM_STATE:2 * SSM_GROUPS * SSM_STATE]

    lane = lax.broadcasted_iota(jnp.int32, (1, LANES), 1)
    a2 = jnp.where(lane < SSM_HEADS, -LOG2_E * jnp.exp(alog_ref[...]), 0.0)
    dt_pre = dt_raw + dtb_ref[...]
    dt = jnp.maximum(dt_pre, 0.0) + jnp.log(1.0 + jnp.exp(-jnp.abs(dt_pre)))
    da = dt * a2
    rows = lax.broadcasted_iota(jnp.int32, (L, L), 0)
    cols = lax.broadcasted_iota(jnp.int32, (L, L), 1)
    causal = rows >= cols
    not_causal = jnp.where(causal, 0.0, -jnp.inf)
    a_cum = _dot3(da, causal.astype(BF16), lhs=False)
    src_side = (a_cum - jnp.log2(dt)).T
    a_last = a_cum[L - 1:L, :]
    expand = expand_ref[...]
    decay_in = jnp.dot(jnp.exp2(a_cum).astype(BF16), expand, preferred_element_type=F32)
    w_end = jnp.dot((jnp.exp2(a_last - a_cum) * dt).astype(BF16), expand, preferred_element_type=F32)
    chunk_decay = _dot3(jnp.broadcast_to(jnp.exp2(a_last), (SUBLANES, LANES)), expand)[0:1, :]

    xw = (xs * w_end).astype(BF16)
    lane_in_pair = lax.broadcasted_iota(jnp.int32, (L, LANES), 1)
    y_parts = []
    for g in range(SSM_GROUPS):
        bg = b_in[:, g * SSM_STATE:(g + 1) * SSM_STATE]
        cg = c_in[:, g * SSM_STATE:(g + 1) * SSM_STATE].astype(BF16)
        cb = lax.dot_general(cg, bg.astype(BF16), (((1,), (1,)), ((), ())),
                             preferred_element_type=F32)
        gcols = slice(g * GROUP_WIDTH, (g + 1) * GROUP_WIDTH)
        st = state_ref[g]
        y_off = jnp.dot(cg, st.astype(BF16), preferred_element_type=F32) * decay_in[:, gcols]
        st_new = jnp.dot(bg.T.astype(BF16), xw[:, gcols], preferred_element_type=F32)
        state_ref[g] = st * chunk_decay[:, gcols] + st_new
        heads_per_group = SSM_HEADS // SSM_GROUPS
        for pair in range(heads_per_group // 2):
            h0 = g * heads_per_group + 2 * pair
            slab = xs[:, h0 * SSM_HEAD_DIM:(h0 + 2) * SSM_HEAD_DIM]
            mats = []
            for hh in (h0, h0 + 1):
                seg = (a_cum[:, hh:hh + 1] - src_side[hh:hh + 1, :]) + not_causal
                mats.append((cb * jnp.exp2(seg)).astype(BF16))
            lhs = jnp.concatenate(mats, axis=1)
            rhs = jnp.concatenate(
                [jnp.where(lane_in_pair < SSM_HEAD_DIM, slab, 0.0),
                 jnp.where(lane_in_pair >= SSM_HEAD_DIM, slab, 0.0)], axis=0).astype(BF16)
            y_diag = jnp.dot(lhs, rhs, preferred_element_type=F32)
            lo = (h0 - g * heads_per_group) * SSM_HEAD_DIM
            y_parts.append(y_diag + y_off[:, lo:lo + LANES])
    y = jnp.concatenate(y_parts, axis=1) + xs * dskip_ref[...]
    y = y * _silu(z.astype(F32))
    outs = []
    for g in range(SSM_GROUPS):
        yg = y[:, g * GROUP_WIDTH:(g + 1) * GROUP_WIDTH]
        ms = jnp.mean(yg * yg, axis=-1, keepdims=True)
        outs.append(yg * lax.rsqrt(ms + EPS))
    return (jnp.concatenate(outs, axis=1) * gain_ref[...]).astype(BF16)


def _ssd_kernel(z_ref, x0_ref, x1_ref, bc_ref, p0_ref, p1_ref, pbc_ref, dt_ref, cw_ref, cb_ref, dtb_ref,
                alog_ref, dskip_ref, gain_ref, expand_ref, *rest):
    n_weights = (len(rest) - 2) // 2
    w_refs, o_ref, w_outs, state_ref = (rest[:n_weights], rest[n_weights], rest[n_weights + 1:-1], rest[-1])
    c = pl.program_id(1)
    L = SSD_CHUNK

    for src, dst in zip(w_refs, w_outs):
        dst[...] = src[...].astype(BF16)

    @pl.when(c == 0)
    def _():
        state_ref[...] = jnp.zeros_like(state_ref)

    cur_refs = (x0_ref, x1_ref, bc_ref)
    prev = tuple(r[...] for r in (p0_ref, p1_ref, pbc_ref))
    first = c == 0
    for k in range(SSD_CHUNKS_PER_STEP):
        rows = slice(k * L, (k + 1) * L)
        cur = tuple(r[rows, :] for r in cur_refs)
        o_ref[rows, :] = _ssd_chunk(first, z_ref[rows, :], cur, prev, dt_ref[rows, :], cw_ref, cb_ref, dtb_ref,
                                    alog_ref, dskip_ref, gain_ref, expand_ref, state_ref)
        prev = cur
        first = jnp.bool_(False)


def _ssd(proj3, dt3, conv_w, conv_b, dt_bias, a_log, d_skip_x, gain, expand, weights):
    b, s, _ = proj3.shape
    L = SSD_CHUNK
    per_step = SSD_CHUNKS_PER_STEP if s % (SSD_CHUNKS_PER_STEP * L) == 0 else 1
    assert per_step == SSD_CHUNKS_PER_STEP
    rows = per_step * L
    half = SSM_WIDTH // 2
    grid = (b, s // rows)
    const = lambda shape: pl.BlockSpec(shape, lambda bi, c: (0, 0))
    xbc = lambda k: pl.BlockSpec((None, rows, half), lambda bi, c: (bi, c, XBC_OFF // half + k))
    xbc_prev = lambda k: pl.BlockSpec(
        (None, L, half), lambda bi, c: (bi, jnp.maximum(c * per_step - 1, 0), XBC_OFF // half + k))
    steps = grid[0] * grid[1]

    def slab(w, out):
        rows_total, cols = w.shape[1], w.shape[2]
        n = steps if rows_total % (steps * BF16_SUBLANES) == 0 else steps // 2
        assert rows_total % (n * BF16_SUBLANES) == 0
        per = steps // n
        return pl.BlockSpec((None, rows_total // n, cols),
                            lambda bi, c: (out, (bi * grid[1] + c) // per, 0))
    return pl.pallas_call(
        _ssd_kernel,
        grid=grid,
        in_specs=[
            pl.BlockSpec((None, rows, SSM_WIDTH), lambda bi, c: (bi, c, Z_OFF // SSM_WIDTH)),
            xbc(0), xbc(1), xbc(2),
            xbc_prev(0), xbc_prev(1), xbc_prev(2),
            pl.BlockSpec((None, rows, LANES), lambda bi, c: (bi, c, 0)),
            const((CONV_WIDTH, CONV_DIM)),
            const((1, CONV_DIM)),
            const((1, LANES)),
            const((1, LANES)),
            const((1, SSM_WIDTH)),
            const((1, SSM_WIDTH)),
            const((LANES, SSM_WIDTH)),
        ] + [slab(w, layer) for w, layer in weights],
        out_specs=[pl.BlockSpec((None, rows, SSM_WIDTH), lambda bi, c: (bi, c, 0))]
        + [slab(w, 0) for w, _ in weights],
        out_shape=[jax.ShapeDtypeStruct((b, s, SSM_WIDTH), BF16)]
        + [jax.ShapeDtypeStruct((1,) + w.shape[1:], BF16) for w, _ in weights],
        scratch_shapes=[pltpu.VMEM((SSM_GROUPS, SSM_STATE, GROUP_WIDTH), F32)],
        compiler_params=_params("arbitrary", "arbitrary"),
        name="ssd",
    )(proj3, proj3, proj3, proj3, proj3, proj3, proj3, dt3, conv_w, conv_b, dt_bias, a_log, d_skip_x,
      gain, expand, *[w for w, _ in weights])


OUTPROJ_ROW_GROUPS = 2


def _outproj_kernel(x_ref, oa_ref, os_ref, g_ref, w_ref, o_ref):
    rows = x_ref.shape[0] // OUTPROJ_ROW_GROUPS
    for r in range(OUTPROJ_ROW_GROUPS):
        sl = slice(r * rows, (r + 1) * rows)
        a = oa_ref[sl, :].astype(F32)
        ms = jnp.mean(a * a, axis=-1, keepdims=True)
        an = (a * lax.rsqrt(ms + EPS) * g_ref[...]).astype(BF16)
        y = jnp.dot(an, w_ref[0:ATT_WIDTH, :], preferred_element_type=F32)
        y = y + jnp.dot(os_ref[sl, :], w_ref[ATT_WIDTH:ATT_WIDTH + SSM_WIDTH, :], preferred_element_type=F32)
        o_ref[sl, :] = x_ref[sl, :] + y


def _out_proj(x2, o_att, o_ssm, gain, w_out, layer, *, tm):
    t = x2.shape[0]
    return pl.pallas_call(
        _outproj_kernel,
        grid=(t // tm,),
        in_specs=[
            pl.BlockSpec((tm, D_MODEL), lambda i: (i, 0)),
            pl.BlockSpec((tm, ATT_WIDTH), lambda i: (i, 0)),
            pl.BlockSpec((tm, SSM_WIDTH), lambda i: (i, 0)),
            pl.BlockSpec((1, ATT_WIDTH), lambda i: (0, 0)),
            pl.BlockSpec((None, ATT_WIDTH + SSM_WIDTH, D_MODEL), lambda i: (layer, 0, 0),
                         pipeline_mode=pl.Buffered(1)),
        ],
        out_specs=pl.BlockSpec((tm, D_MODEL), lambda i: (i, 0)),
        out_shape=jax.ShapeDtypeStruct((t, D_MODEL), F32),
        compiler_params=_params("arbitrary"),
        name="out_proj",
    )(x2, o_att, o_ssm, gain, w_out)


FFN_FIRST_STEP_ROW_GROUPS = 2


def _ffn_kernel(x_ref, g_ref, wg_ref, wu_ref, wd_ref, o_ref, h_ref):
    f = pl.program_id(1)

    def swiglu(h):
        gate = jnp.dot(h, wg_ref[...], preferred_element_type=F32)
        up = jnp.dot(h, wu_ref[...], preferred_element_type=F32)
        act = (_silu(gate) * up).astype(BF16)
        return jnp.dot(act, wd_ref[...], preferred_element_type=F32)

    @pl.when(f == 0)
    def _():
        rows = x_ref.shape[0] // FFN_FIRST_STEP_ROW_GROUPS
        for r in range(FFN_FIRST_STEP_ROW_GROUPS):
            sl = slice(r * rows, (r + 1) * rows)
            x = x_ref[sl, :]
            ms = jnp.mean(x * x, axis=-1, keepdims=True)
            h = (x * lax.rsqrt(ms + EPS) * g_ref[...]).astype(BF16)
            h_ref[sl, :] = h
            o_ref[sl, :] = x + swiglu(h)

    @pl.when(f > 0)
    def _():
        o_ref[...] += swiglu(h_ref[...])


def _ffn(x2, g, w_gate, w_up, w_down, layer, *, tm, tf):
    t = x2.shape[0]
    return pl.pallas_call(
        _ffn_kernel,
        grid=(t // tm, D_FF // tf),
        in_specs=[
            pl.BlockSpec((tm, D_MODEL), lambda i, f: (i, 0)),
            pl.BlockSpec((1, D_MODEL), lambda i, f: (0, 0)),
            pl.BlockSpec((None, D_MODEL, tf), lambda i, f: (layer, 0, f)),
            pl.BlockSpec((None, D_MODEL, tf), lambda i, f: (layer, 0, f)),
            pl.BlockSpec((None, tf, D_MODEL), lambda i, f: (layer, f, 0)),
        ],
        out_specs=pl.BlockSpec((tm, D_MODEL), lambda i, f: (i, 0)),
        out_shape=jax.ShapeDtypeStruct((t, D_MODEL), F32),
        scratch_shapes=[pltpu.VMEM((tm, D_MODEL), BF16)],
        compiler_params=_params("arbitrary", "arbitrary"),
        name="ffn",
    )(x2, g, w_gate, w_up, w_down)


def _tile(pref, n):
    return pref if n % pref == 0 else n


def _layer(x, layer, norm_mix, q_gain, k_gain, conv_w, conv_b, dt_bias, a_log, d_skip,
           attn_out_gain, ssm_out_gain, norm_ffn, w_main, w_dt, w_out, w_gate, w_up, w_down, expand):
    b, s, _ = x.shape
    t = b * s
    x2 = x.reshape(t, D_MODEL)
    row = lambda v: v.reshape(1, -1).astype(F32)
    pad_lanes = lambda v: jnp.pad(v.astype(F32), (0, LANES - v.shape[0])).reshape(1, LANES)

    proj, dt_raw = _in_proj(x2, row(norm_mix), w_main, w_dt, layer,
                            row(q_gain) * (LOG2_E * ATT_HEAD_DIM ** -0.5), row(k_gain),
                            tm=_tile(512, t), tn=MAIN_DIM)
    proj3 = proj.reshape(b, s, MAIN_DIM)

    o_att = _attention(proj3)
    to_cast = [(w_out, layer), (w_gate, layer), (w_up, layer), (w_down, layer)]
    o_ssm, w_out, w_gate, w_up, w_down = _ssd(
        proj3, dt_raw.reshape(b, s, LANES), conv_w.astype(F32), row(conv_b), pad_lanes(dt_bias), pad_lanes(a_log),
        row(jnp.repeat(d_skip, SSM_HEAD_DIM)), row(ssm_out_gain), expand, to_cast)

    x2 = _out_proj(x2, o_att.reshape(t, ATT_WIDTH), o_ssm.reshape(t, SSM_WIDTH),
                   row(attn_out_gain), w_out, 0, tm=_tile(1024, t))
    x2 = _ffn(x2, row(norm_ffn), w_gate, w_up, w_down, 0, tm=_tile(1024, t), tf=512)
    return x2.reshape(b, s, D_MODEL)


def _inproj_weights(w_in):
    w_main = w_in.astype(BF16)
    w_dt = jnp.pad(w_in[:, :, MAIN_DIM:], ((0, 0), (0, 0), (0, LANES - SSM_HEADS))).astype(BF16)
    return w_main, w_dt


def _head_expand():
    head_of_lane = jnp.arange(SSM_WIDTH, dtype=jnp.int32) // SSM_HEAD_DIM
    return (jnp.arange(LANES, dtype=jnp.int32)[:, None] == head_of_lane[None, :]).astype(BF16)


def kernel(x, norm_mix, w_in, q_gain, k_gain, conv_w, conv_b, dt_bias, a_log, d_skip, attn_out_gain, ssm_out_gain, w_out, norm_ffn, w_gate, w_up, w_down):
    w_main, w_dt = _inproj_weights(w_in)
    expand = _head_expand()
    for i in range(norm_mix.shape[0]):
        x = _layer(x, i, norm_mix[i], q_gain[i], k_gain[i], conv_w[i], conv_b[i], dt_bias[i], a_log[i],
                   d_skip[i], attn_out_gain[i], ssm_out_gain[i], norm_ffn[i], w_main, w_dt,
                   w_out, w_gate, w_up, w_down, expand)
    return x
```

```python
import functools

import jax
import jax.numpy as jnp
from jax import lax
from jax.experimental import pallas as pl
from jax.experimental.pallas import tpu as pltpu

F32 = jnp.float32
BF16 = jnp.bfloat16

D_MODEL = 2048
ATT_WIDTH = 1024
SSM_WIDTH = 1024
ATT_HEAD_DIM = 128
ATT_HEADS = ATT_WIDTH // ATT_HEAD_DIM
SSM_HEAD_DIM = 64
SSM_HEADS = SSM_WIDTH // SSM_HEAD_DIM
SSM_GROUPS = 2
SSM_STATE = 128
GROUP_WIDTH = SSM_WIDTH // SSM_GROUPS
CONV_WIDTH = 4
CONV_DIM = SSM_WIDTH + 2 * SSM_GROUPS * SSM_STATE
SSD_CHUNK = 128
D_FF = 5632
EPS = 1e-6
LOG2_E = 1.4426950408889634

LANES = 128
SUBLANES = 8
BF16_SUBLANES = 16
VMEM_LIMIT = 56 * 1024 * 1024

MAIN_DIM = 3 * ATT_WIDTH + SSM_WIDTH + CONV_DIM
Q_OFF, K_OFF, V_OFF = 0, ATT_WIDTH, 2 * ATT_WIDTH
Z_OFF = 3 * ATT_WIDTH
XBC_OFF = Z_OFF + SSM_WIDTH


def _params(*sem):
    return pltpu.CompilerParams(dimension_semantics=sem, vmem_limit_bytes=VMEM_LIMIT)


def _silu(x):
    h = 0.5 * x
    return h + h * jnp.tanh(h)


INPROJ_SUB = 512


def _inproj_kernel(x_ref, g_ref, w_ref, wdt_ref, qg_ref, kg_ref, o_ref, dt_ref, h_ref, *, tn):
    j = pl.program_id(1)
    assert tn >= K_OFF + ATT_WIDTH

    def columns(first_tile):
        h = h_ref[...]
        for lo in range(0, tn, INPROJ_SUB):
            hi = min(lo + INPROJ_SUB, tn)
            acc = jnp.dot(h, w_ref[:, lo:hi], preferred_element_type=F32)
            for c0 in range(lo, hi, ATT_HEAD_DIM):
                seg = acc[:, c0 - lo:c0 - lo + ATT_HEAD_DIM]
                if first_tile and c0 < K_OFF + ATT_WIDTH:
                    gain = qg_ref[...] if c0 < K_OFF else kg_ref[...]
                    ms = jnp.mean(seg * seg, axis=-1, keepdims=True)
                    seg = seg * lax.rsqrt(ms + EPS) * gain
                o_ref[:, c0:c0 + ATT_HEAD_DIM] = seg.astype(BF16)

    @pl.when(j == 0)
    def _():
        x = x_ref[...]
        ms = jnp.mean(x * x, axis=-1, keepdims=True)
        h = (x * lax.rsqrt(ms + EPS) * g_ref[...]).astype(BF16)
        h_ref[...] = h
        dt_ref[...] = jnp.dot(h, wdt_ref[...], preferred_element_type=F32)
        columns(True)

    @pl.when(j > 0)
    def _():
        columns(False)


def _in_proj(x2, g, w_in, layer, qg, kg, *, tm, tn):
    t = x2.shape[0]
    grid = (t // tm, MAIN_DIM // tn)
    return pl.pallas_call(
        functools.partial(_inproj_kernel, tn=tn),
        grid=grid,
        in_specs=[
            pl.BlockSpec((tm, D_MODEL), lambda i, j: (i, 0)),
            pl.BlockSpec((1, D_MODEL), lambda i, j: (0, 0)),
            pl.BlockSpec((None, D_MODEL, tn), lambda i, j: (layer, 0, j),
                         pipeline_mode=pl.Buffered(1) if tn == MAIN_DIM else None),
            pl.BlockSpec((None, D_MODEL, LANES), lambda i, j: (layer, 0, MAIN_DIM // LANES)),
            pl.BlockSpec((1, ATT_HEAD_DIM), lambda i, j: (0, 0)),
            pl.BlockSpec((1, ATT_HEAD_DIM), lambda i, j: (0, 0)),
        ],
        out_specs=[
            pl.BlockSpec((tm, tn), lambda i, j: (i, j)),
            pl.BlockSpec((tm, LANES), lambda i, j: (i, 0)),
        ],
        out_shape=[
            jax.ShapeDtypeStruct((t, MAIN_DIM), BF16),
            jax.ShapeDtypeStruct((t, LANES), F32),
        ],
        scratch_shapes=[pltpu.VMEM((tm, D_MODEL), BF16)],
        compiler_params=_params("arbitrary", "arbitrary"),
        name="in_proj",
    )(x2, g, w_in, w_in, qg, kg)


ATT_TILE = 256
ATT_HEADS_PER_STEP = 8
LOG2_WEIGHT_FLOOR = -152.0


def _attn_kernel(q_ref, k_ref, v_ref, o_ref, r_ref, acc_ref):
    t = ATT_TILE
    heads = ATT_HEADS_PER_STEP
    rows = lax.broadcasted_iota(jnp.int32, (t, t), 0)
    cols = lax.broadcasted_iota(jnp.int32, (t, t), 1)
    later_sum = (rows > cols).astype(BF16)
    causal = cols < rows
    sign_bit = jnp.uint32(0x80000000)
    lanes = [slice(hh * ATT_HEAD_DIM, (hh + 1) * ATT_HEAD_DIM) for hh in range(heads)]

    def step(i, blocks, start):
        qs = pl.multiple_of(i * t, t)
        starts = [pl.multiple_of(jblk * t, t) for jblk, _ in blocks]
        log_betas, log_remains, ws = [None] * heads, [None] * heads, [None] * heads
        alive = None
        for s in range(heads + 2):
            if s < heads:
                log_betas[s], log_remains[s] = [], []
                for ks, (_, diagonal) in zip(starts, blocks):
                    z = lax.dot_general(q_ref[pl.ds(qs, t), lanes[s]], k_ref[pl.ds(ks, t), lanes[s]],
                                        (((1,), (1,)), ((), ())), preferred_element_type=F32)
                    neg_abs = lax.bitcast_convert_type(lax.bitcast_convert_type(z, jnp.uint32) | sign_bit, F32)
                    log_beta = jnp.minimum(z, 0.0) - jnp.log2(1.0 + jnp.exp2(neg_abs))
                    log_remain = log_beta - z
                    if diagonal:
                        log_remain = jnp.where(causal, log_remain, 0.0)
                    log_betas[s].append(log_beta)
                    log_remains[s].append(log_remain)
            if 0 <= s - 1 < heads:
                hh = s - 1
                ws[hh] = []
                for p, (_, diagonal) in enumerate(blocks):
                    within = jnp.dot(log_remains[hh][p].astype(BF16), later_sum, preferred_element_type=F32)
                    w = jnp.exp2(log_betas[hh][p] + within)
                    if diagonal:
                        w = jnp.where(causal, w, 0.0)
                    ws[hh].append(w.astype(BF16))
            if 0 <= s - 2 < heads:
                hh = s - 2
                r, acc = (None, None) if start else (r_ref[hh], acc_ref[hh])
                for p, ks in enumerate(starts):
                    pv = jnp.dot(ws[hh][p], v_ref[pl.ds(ks, t), lanes[hh]], preferred_element_type=F32)
                    block_sum = jnp.sum(log_remains[hh][p], axis=1, keepdims=True)
                    if r is None:
                        r, acc = block_sum, pv
                    else:
                        r, acc = r + block_sum, acc + jnp.exp2(r) * pv
                r_ref[hh] = r
                acc_ref[hh] = acc
                alive = r if alive is None else jnp.maximum(alive, r)
        return (jnp.max(alive) > LOG2_WEIGHT_FLOOR).astype(jnp.int32)

    def finish(i):
        for hh in range(heads):
            o_ref[pl.ds(pl.multiple_of(i * t, t), t), lanes[hh]] = acc_ref[hh].astype(o_ref.dtype)

    step(0, ((0, True),), True)
    finish(0)

    def query_block(i, carry):
        go = step(i, ((i, True), (i - 1, False)), True)

        def cond(c):
            n, go = c
            return jnp.logical_and(n < i, go > 0)

        def body(c):
            n, _ = c
            return n + 1, step(i, ((i - 1 - n, False),), False)

        lax.while_loop(cond, body, (jnp.int32(1), go))
        finish(i)
        return carry

    lax.fori_loop(1, q_ref.shape[0] // t, query_block, 0)


def _attention(proj3):
    b, s, _ = proj3.shape
    width = ATT_HEADS_PER_STEP * ATT_HEAD_DIM
    grid = (b, ATT_HEADS // ATT_HEADS_PER_STEP)
    qb, kb, vb = Q_OFF // width, K_OFF // width, V_OFF // width
    return pl.pallas_call(
        _attn_kernel,
        grid=grid,
        in_specs=[
            pl.BlockSpec((None, s, width), lambda bi, h: (bi, 0, qb + h)),
            pl.BlockSpec((None, s, width), lambda bi, h: (bi, 0, kb + h)),
            pl.BlockSpec((None, s, width), lambda bi, h: (bi, 0, vb + h)),
        ],
        out_specs=pl.BlockSpec((None, s, width), lambda bi, h: (bi, 0, h)),
        out_shape=jax.ShapeDtypeStruct((b, s, ATT_WIDTH), BF16),
        scratch_shapes=[
            pltpu.VMEM((ATT_HEADS_PER_STEP, ATT_TILE, 1), F32),
            pltpu.VMEM((ATT_HEADS_PER_STEP, ATT_TILE, ATT_HEAD_DIM), F32),
        ],
        compiler_params=_params("arbitrary", "arbitrary"),
        name="attn",
    )(proj3, proj3, proj3)


def _split3(x):
    h1 = x.astype(BF16)
    r1 = x - h1.astype(F32)
    h2 = r1.astype(BF16)
    h3 = (r1 - h2.astype(F32)).astype(BF16)
    return h1, h2, h3


def _dot3(x, m, lhs=True):
    parts = _split3(x)
    if lhs:
        outs = [jnp.dot(p, m, preferred_element_type=F32) for p in parts]
    else:
        outs = [jnp.dot(m, p, preferred_element_type=F32) for p in parts]
    return outs[0] + outs[1] + outs[2]


SSD_CHUNKS_PER_STEP = 8


def _ssd_chunk(first, z, cur, prev, dt_raw, cw_ref, cb_ref, dtb_ref, alog_ref, dskip_ref, gain_ref,
               expand_ref, state_ref):
    L = SSD_CHUNK
    out_t = lax.broadcasted_iota(jnp.int32, ((CONV_WIDTH - 1) * L, 2 * L), 0)
    src_t = lax.broadcasted_iota(jnp.int32, ((CONV_WIDTH - 1) * L, 2 * L), 1)
    shift = out_t // L + 1
    hit = src_t == L + (out_t - (shift - 1) * L) - shift
    shifts = jnp.logical_and(hit, jnp.logical_or(src_t >= L, jnp.logical_not(first))).astype(BF16)
    pieces = []
    col = 0
    for cur_blk, prev_blk in zip(cur, prev):
        width = cur_blk.shape[1]
        both = jnp.concatenate([prev_blk, cur_blk], axis=0)
        back = jnp.dot(shifts, both, preferred_element_type=F32)
        acc = cb_ref[:, col:col + width] + cur_blk.astype(F32) * cw_ref[CONV_WIDTH - 1:CONV_WIDTH, col:col + width]
        for d in range(CONV_WIDTH - 1):
            tap = CONV_WIDTH - 2 - d
            acc = acc + back[d * L:(d + 1) * L] * cw_ref[tap:tap + 1, col:col + width]
        pieces.append(_silu(acc))
        col += width
    xs = jnp.concatenate(pieces[0:2], axis=1)
    b_in = pieces[2][:, 0:SSM_GROUPS * SSM_STATE]
    c_in = pieces[2][:, SSM_GROUPS * SSM_STATE:2 * SSM_GROUPS * SSM_STATE]

    lane = lax.broadcasted_iota(jnp.int32, (1, LANES), 1)
    a2 = jnp.where(lane < SSM_HEADS, -LOG2_E * jnp.exp(alog_ref[...]), 0.0)
    dt_pre = dt_raw + dtb_ref[...]
    dt = jnp.maximum(dt_pre, 0.0) + jnp.log(1.0 + jnp.exp(-jnp.abs(dt_pre)))
    da = dt * a2
    rows = lax.broadcasted_iota(jnp.int32, (L, L), 0)
    cols = lax.broadcasted_iota(jnp.int32, (L, L), 1)
    causal = rows >= cols
    not_causal = jnp.where(causal, 0.0, -jnp.inf)
    a_cum = _dot3(da, causal.astype(BF16), lhs=False)
    src_side = (a_cum - jnp.log2(dt)).T
    a_last = a_cum[L - 1:L, :]
    expand = expand_ref[...]
    decay_in = jnp.dot(jnp.exp2(a_cum).astype(BF16), expand, preferred_element_type=F32)
    w_end = jnp.dot((jnp.exp2(a_last - a_cum) * dt).astype(BF16), expand, preferred_element_type=F32)
    chunk_decay = _dot3(jnp.broadcast_to(jnp.exp2(a_last), (SUBLANES, LANES)), expand)[0:1, :]

    xw = (xs * w_end).astype(BF16)
    lane_in_pair = lax.broadcasted_iota(jnp.int32, (L, LANES), 1)
    y_parts = []
    for g in range(SSM_GROUPS):
        bg = b_in[:, g * SSM_STATE:(g + 1) * SSM_STATE]
        cg = c_in[:, g * SSM_STATE:(g + 1) * SSM_STATE].astype(BF16)
        cb = lax.dot_general(cg, bg.astype(BF16), (((1,), (1,)), ((), ())),
                             preferred_element_type=F32)
        gcols = slice(g * GROUP_WIDTH, (g + 1) * GROUP_WIDTH)
        st = state_ref[g]
        y_off = jnp.dot(cg, st.astype(BF16), preferred_element_type=F32) * decay_in[:, gcols]
        st_new = jnp.dot(bg.T.astype(BF16), xw[:, gcols], preferred_element_type=F32)
        state_ref[g] = st * chunk_decay[:, gcols] + st_new
        heads_per_group = SSM_HEADS // SSM_GROUPS
        for pair in range(heads_per_group // 2):
            h0 = g * heads_per_group + 2 * pair
            slab = xs[:, h0 * SSM_HEAD_DIM:(h0 + 2) * SSM_HEAD_DIM]
            mats = []
            for hh in (h0, h0 + 1):
                seg = (a_cum[:, hh:hh + 1] - src_side[hh:hh + 1, :]) + not_causal
                mats.append((cb * jnp.exp2(seg)).astype(BF16))
            lhs = jnp.concatenate(mats, axis=1)
            rhs = jnp.concatenate(
                [jnp.where(lane_in_pair < SSM_HEAD_DIM, slab, 0.0),
                 jnp.where(lane_in_pair >= SSM_HEAD_DIM, slab, 0.0)], axis=0).astype(BF16)
            y_diag = jnp.dot(lhs, rhs, preferred_element_type=F32)
            lo = (h0 - g * heads_per_group) * SSM_HEAD_DIM
            y_parts.append(y_diag + y_off[:, lo:lo + LANES])
    y = jnp.concatenate(y_parts, axis=1) + xs * dskip_ref[...]
    y = y * _silu(z.astype(F32))
    outs = []
    for g in range(SSM_GROUPS):
        yg = y[:, g * GROUP_WIDTH:(g + 1) * GROUP_WIDTH]
        ms = jnp.mean(yg * yg, axis=-1, keepdims=True)
        outs.append(yg * lax.rsqrt(ms + EPS))
    return (jnp.concatenate(outs, axis=1) * gain_ref[...]).astype(BF16)


def _ssd_kernel(z_ref, x0_ref, x1_ref, bc_ref, p0_ref, p1_ref, pbc_ref, dt_ref, cw_ref, cb_ref, dtb_ref,
                alog_ref, dskip_ref, gain_ref, expand_ref, *rest, group_sizes):
    n_weights = sum(group_sizes)
    w_refs, o_ref, w_outs, state_ref = (rest[:n_weights], rest[n_weights], rest[n_weights + 1:-1], rest[-1])
    c = pl.program_id(1)
    L = SSD_CHUNK

    first = 0
    for size, dst in zip(group_sizes, w_outs):
        for k in range(size):
            dst[k] = w_refs[first + k][...].astype(BF16)
        first += size

    @pl.when(c == 0)
    def _():
        state_ref[...] = jnp.zeros_like(state_ref)

    cur_refs = (x0_ref, x1_ref, bc_ref)
    prev = tuple(r[...] for r in (p0_ref, p1_ref, pbc_ref))
    first = c == 0
    for k in range(SSD_CHUNKS_PER_STEP):
        rows = slice(k * L, (k + 1) * L)
        cur = tuple(r[rows, :] for r in cur_refs)
        o_ref[rows, :] = _ssd_chunk(first, z_ref[rows, :], cur, prev, dt_ref[rows, :], cw_ref, cb_ref, dtb_ref,
                                    alog_ref, dskip_ref, gain_ref, expand_ref, state_ref)
        prev = cur
        first = jnp.bool_(False)


def _ssd(proj3, dt3, conv_w, conv_b, dt_bias, a_log, d_skip_x, gain, expand, layer, groups):
    b, s, _ = proj3.shape
    L = SSD_CHUNK
    per_step = SSD_CHUNKS_PER_STEP if s % (SSD_CHUNKS_PER_STEP * L) == 0 else 1
    assert per_step == SSD_CHUNKS_PER_STEP
    rows = per_step * L
    half = SSM_WIDTH // 2
    grid = (b, s // rows)
    const = lambda shape: pl.BlockSpec(shape, lambda bi, c: (0, 0))
    xbc = lambda k: pl.BlockSpec((None, rows, half), lambda bi, c: (bi, c, XBC_OFF // half + k))
    xbc_prev = lambda k: pl.BlockSpec(
        (None, L, half), lambda bi, c: (bi, jnp.maximum(c * per_step - 1, 0), XBC_OFF // half + k))
    steps = grid[0] * grid[1]

    def slab(w, members):
        rows_total, cols = w.shape[1], w.shape[2]
        n = steps if rows_total % (steps * BF16_SUBLANES) == 0 else steps // 2
        assert rows_total % (n * BF16_SUBLANES) == 0
        per = steps // n
        lead, pick = (None, layer) if members is None else (members, 0)
        return pl.BlockSpec((lead, rows_total // n, cols),
                            lambda bi, c: (pick, (bi * grid[1] + c) // per, 0))
    return pl.pallas_call(
        functools.partial(_ssd_kernel, group_sizes=tuple(len(g) for g in groups)),
        grid=grid,
        in_specs=[
            pl.BlockSpec((None, rows, SSM_WIDTH), lambda bi, c: (bi, c, Z_OFF // SSM_WIDTH)),
            xbc(0), xbc(1), xbc(2),
            xbc_prev(0), xbc_prev(1), xbc_prev(2),
            pl.BlockSpec((None, rows, LANES), lambda bi, c: (bi, c, 0)),
            const((CONV_WIDTH, CONV_DIM)),
            const((1, CONV_DIM)),
            const((1, LANES)),
            const((1, LANES)),
            const((1, SSM_WIDTH)),
            const((1, SSM_WIDTH)),
            const((LANES, SSM_WIDTH)),
        ] + [slab(w, None) for g in groups for w in g],
        out_specs=[pl.BlockSpec((None, rows, SSM_WIDTH), lambda bi, c: (bi, c, 0))]
        + [slab(g[0], len(g)) for g in groups],
        out_shape=[jax.ShapeDtypeStruct((b, s, SSM_WIDTH), BF16)]
        + [jax.ShapeDtypeStruct((len(g),) + g[0].shape[1:], BF16) for g in groups],
        scratch_shapes=[pltpu.VMEM((SSM_GROUPS, SSM_STATE, GROUP_WIDTH), F32)],
        compiler_params=_params("arbitrary", "arbitrary"),
        name="ssd",
    )(proj3, proj3, proj3, proj3, proj3, proj3, proj3, dt3, conv_w, conv_b, dt_bias, a_log, d_skip_x,
      gain, expand, *[w for g in groups for w in g])


OUTPROJ_ROW_GROUPS = 2


def _outproj_kernel(x_ref, oa_ref, os_ref, g_ref, w_ref, o_ref):
    rows = x_ref.shape[0] // OUTPROJ_ROW_GROUPS
    for r in range(OUTPROJ_ROW_GROUPS):
        sl = slice(r * rows, (r + 1) * rows)
        a = oa_ref[sl, :].astype(F32)
        ms = jnp.mean(a * a, axis=-1, keepdims=True)
        an = (a * lax.rsqrt(ms + EPS) * g_ref[...]).astype(BF16)
        y = jnp.dot(an, w_ref[0:ATT_WIDTH, :], preferred_element_type=F32)
        y = y + jnp.dot(os_ref[sl, :], w_ref[ATT_WIDTH:ATT_WIDTH + SSM_WIDTH, :], preferred_element_type=F32)
        o_ref[sl, :] = x_ref[sl, :] + y


def _out_proj(x2, o_att, o_ssm, gain, w_out, layer, *, tm):
    t = x2.shape[0]
    return pl.pallas_call(
        _outproj_kernel,
        grid=(t // tm,),
        in_specs=[
            pl.BlockSpec((tm, D_MODEL), lambda i: (i, 0)),
            pl.BlockSpec((tm, ATT_WIDTH), lambda i: (i, 0)),
            pl.BlockSpec((tm, SSM_WIDTH), lambda i: (i, 0)),
            pl.BlockSpec((1, ATT_WIDTH), lambda i: (0, 0)),
            pl.BlockSpec((None, ATT_WIDTH + SSM_WIDTH, D_MODEL), lambda i: (layer, 0, 0),
                         pipeline_mode=pl.Buffered(1)),
        ],
        out_specs=pl.BlockSpec((tm, D_MODEL), lambda i: (i, 0)),
        out_shape=jax.ShapeDtypeStruct((t, D_MODEL), F32),
        compiler_params=_params("arbitrary"),
        name="out_proj",
    )(x2, o_att, o_ssm, gain, w_out)


FFN_FIRST_STEP_ROW_GROUPS = 2


def _ffn_kernel(x_ref, g_ref, wgu_ref, wd_ref, o_ref, h_ref):
    f = pl.program_id(1)

    def swiglu(h):
        gate = jnp.dot(h, wgu_ref[0], preferred_element_type=F32)
        up = jnp.dot(h, wgu_ref[1], preferred_element_type=F32)
        act = (_silu(gate) * up).astype(BF16)
        return jnp.dot(act, wd_ref[...], preferred_element_type=F32)

    @pl.when(f == 0)
    def _():
        rows = x_ref.shape[0] // FFN_FIRST_STEP_ROW_GROUPS
        for r in range(FFN_FIRST_STEP_ROW_GROUPS):
            sl = slice(r * rows, (r + 1) * rows)
            x = x_ref[sl, :]
            ms = jnp.mean(x * x, axis=-1, keepdims=True)
            h = (x * lax.rsqrt(ms + EPS) * g_ref[...]).astype(BF16)
            h_ref[sl, :] = h
            o_ref[sl, :] = x + swiglu(h)

    @pl.when(f > 0)
    def _():
        o_ref[...] += swiglu(h_ref[...])


def _ffn(x2, g, w_gate_up, w_down, *, tm, tf):
    t = x2.shape[0]
    return pl.pallas_call(
        _ffn_kernel,
        grid=(t // tm, D_FF // tf),
        in_specs=[
            pl.BlockSpec((tm, D_MODEL), lambda i, f: (i, 0)),
            pl.BlockSpec((1, D_MODEL), lambda i, f: (0, 0)),
            pl.BlockSpec((2, D_MODEL, tf), lambda i, f: (0, 0, f)),
            pl.BlockSpec((None, tf, D_MODEL), lambda i, f: (0, f, 0)),
        ],
        out_specs=pl.BlockSpec((tm, D_MODEL), lambda i, f: (i, 0)),
        out_shape=jax.ShapeDtypeStruct((t, D_MODEL), F32),
        scratch_shapes=[pltpu.VMEM((tm, D_MODEL), BF16)],
        compiler_params=_params("arbitrary", "arbitrary"),
        name="ffn",
    )(x2, g, w_gate_up, w_down)


def _tile(pref, n):
    return pref if n % pref == 0 else n


def _layer(x, layer, norm_mix, q_gain, k_gain, conv_w, conv_b, dt_bias, a_log, d_skip,
           attn_out_gain, ssm_out_gain, norm_ffn, w_in, w_out, w_gate, w_up, w_down, expand):
    b, s, _ = x.shape
    t = b * s
    x2 = x.reshape(t, D_MODEL)
    row = lambda v: v.reshape(1, -1).astype(F32)
    pad_lanes = lambda v: jnp.pad(v.astype(F32), (0, LANES - v.shape[0])).reshape(1, LANES)

    proj, dt_raw = _in_proj(x2, row(norm_mix), w_in, layer,
                            row(q_gain) * (LOG2_E * ATT_HEAD_DIM ** -0.5), row(k_gain),
                            tm=_tile(512, t), tn=MAIN_DIM)
    proj3 = proj.reshape(b, s, MAIN_DIM)

    o_att = _attention(proj3)
    o_ssm, w_out, w_gate_up, w_down = _ssd(
        proj3, dt_raw.reshape(b, s, LANES), conv_w.astype(F32), row(conv_b), pad_lanes(dt_bias), pad_lanes(a_log),
        row(jnp.repeat(d_skip, SSM_HEAD_DIM)), row(ssm_out_gain), expand, layer,
        ((w_out,), (w_gate, w_up), (w_down,)))

    x2 = _out_proj(x2, o_att.reshape(t, ATT_WIDTH), o_ssm.reshape(t, SSM_WIDTH),
                   row(attn_out_gain), w_out, 0, tm=_tile(1024, t))
    x2 = _ffn(x2, row(norm_ffn), w_gate_up, w_down, tm=_tile(1024, t), tf=512)
    return x2.reshape(b, s, D_MODEL)


def _inproj_weights(w_in):
    pad = -w_in.shape[2] % LANES
    assert w_in.shape[2] + pad == MAIN_DIM + LANES
    return jnp.pad(w_in, ((0, 0), (0, 0), (0, pad))).astype(BF16)


def _head_expand():
    head_of_lane = jnp.arange(SSM_WIDTH, dtype=jnp.int32) // SSM_HEAD_DIM
    return (jnp.arange(LANES, dtype=jnp.int32)[:, None] == head_of_lane[None, :]).astype(BF16)


def kernel(x, norm_mix, w_in, q_gain, k_gain, conv_w, conv_b, dt_bias, a_log, d_skip, attn_out_gain, ssm_out_gain, w_out, norm_ffn, w_gate, w_up, w_down):
    w_in = _inproj_weights(w_in)
    expand = _head_expand()
    for i in range(norm_mix.shape[0]):
        x = _layer(x, i, norm_mix[i], q_gain[i], k_gain[i], conv_w[i], conv_b[i], dt_bias[i], a_log[i],
                   d_skip[i], attn_out_gain[i], ssm_out_gain[i], norm_ffn[i], w_in,
                   w_out, w_gate, w_up, w_down, expand)
    return x
```

```python
import functools

import jax
import jax.numpy as jnp
from jax import lax
from jax.experimental import pallas as pl
from jax.experimental.pallas import tpu as pltpu

F32 = jnp.float32
BF16 = jnp.bfloat16

D_MODEL = 2048
ATT_WIDTH = 1024
SSM_WIDTH = 1024
ATT_HEAD_DIM = 128
ATT_HEADS = ATT_WIDTH // ATT_HEAD_DIM
SSM_HEAD_DIM = 64
SSM_HEADS = SSM_WIDTH // SSM_HEAD_DIM
SSM_GROUPS = 2
SSM_STATE = 128
GROUP_WIDTH = SSM_WIDTH // SSM_GROUPS
CONV_WIDTH = 4
CONV_DIM = SSM_WIDTH + 2 * SSM_GROUPS * SSM_STATE
SSD_CHUNK = 128
D_FF = 5632
EPS = 1e-6
LOG2_E = 1.4426950408889634

LANES = 128
SUBLANES = 8
BF16_SUBLANES = 16
VMEM_LIMIT = 56 * 1024 * 1024

MAIN_DIM = 3 * ATT_WIDTH + SSM_WIDTH + CONV_DIM
Q_OFF, K_OFF, V_OFF = 0, ATT_WIDTH, 2 * ATT_WIDTH
Z_OFF = 3 * ATT_WIDTH
XBC_OFF = Z_OFF + SSM_WIDTH


def _params(*sem):
    return pltpu.CompilerParams(dimension_semantics=sem, vmem_limit_bytes=VMEM_LIMIT)


def _silu(x):
    h = 0.5 * x
    return h + h * jnp.tanh(h)


INPROJ_SUB = 512


def _inproj_kernel(x_ref, g_ref, w_ref, wdt_ref, qg_ref, kg_ref, o_ref, dt_ref, h_ref, *, tn):
    j = pl.program_id(1)
    assert tn >= K_OFF + ATT_WIDTH

    def columns(first_tile):
        h = h_ref[...]
        for lo in range(0, tn, INPROJ_SUB):
            hi = min(lo + INPROJ_SUB, tn)
            acc = jnp.dot(h, w_ref[:, lo:hi], preferred_element_type=F32)
            for c0 in range(lo, hi, ATT_HEAD_DIM):
                seg = acc[:, c0 - lo:c0 - lo + ATT_HEAD_DIM]
                if first_tile and c0 < K_OFF + ATT_WIDTH:
                    gain = qg_ref[...] if c0 < K_OFF else kg_ref[...]
                    ms = jnp.mean(seg * seg, axis=-1, keepdims=True)
                    seg = seg * lax.rsqrt(ms + EPS) * gain
                o_ref[:, c0:c0 + ATT_HEAD_DIM] = seg.astype(BF16)

    @pl.when(j == 0)
    def _():
        x = x_ref[...]
        ms = jnp.mean(x * x, axis=-1, keepdims=True)
        h = (x * lax.rsqrt(ms + EPS) * g_ref[...]).astype(BF16)
        h_ref[...] = h
        dt_ref[...] = jnp.dot(h, wdt_ref[...], preferred_element_type=F32)
        columns(True)

    @pl.when(j > 0)
    def _():
        columns(False)


def _in_proj(x2, g, w_main, w_dt, layer, qg, kg, *, tm, tn):
    t = x2.shape[0]
    grid = (t // tm, MAIN_DIM // tn)
    return pl.pallas_call(
        functools.partial(_inproj_kernel, tn=tn),
        grid=grid,
        in_specs=[
            pl.BlockSpec((tm, D_MODEL), lambda i, j: (i, 0)),
            pl.BlockSpec((1, D_MODEL), lambda i, j: (0, 0)),
            pl.BlockSpec((None, D_MODEL, tn), lambda i, j: (layer, 0, j),
                         pipeline_mode=pl.Buffered(1) if tn == MAIN_DIM else None),
            pl.BlockSpec((None, D_MODEL, LANES), lambda i, j: (layer, 0, 0)),
            pl.BlockSpec((1, ATT_HEAD_DIM), lambda i, j: (0, 0)),
            pl.BlockSpec((1, ATT_HEAD_DIM), lambda i, j: (0, 0)),
        ],
        out_specs=[
            pl.BlockSpec((tm, tn), lambda i, j: (i, j)),
            pl.BlockSpec((tm, LANES), lambda i, j: (i, 0)),
        ],
        out_shape=[
            jax.ShapeDtypeStruct((t, MAIN_DIM), BF16),
            jax.ShapeDtypeStruct((t, LANES), F32),
        ],
        scratch_shapes=[pltpu.VMEM((tm, D_MODEL), BF16)],
        compiler_params=_params("arbitrary", "arbitrary"),
        name="in_proj",
    )(x2, g, w_main, w_dt, qg, kg)


ATT_TILE = 256
ATT_HEADS_PER_STEP = 8
LOG2_WEIGHT_FLOOR = -152.0


def _attn_kernel(q_ref, k_ref, v_ref, o_ref, r_ref, acc_ref):
    t = ATT_TILE
    heads = ATT_HEADS_PER_STEP
    rows = lax.broadcasted_iota(jnp.int32, (t, t), 0)
    cols = lax.broadcasted_iota(jnp.int32, (t, t), 1)
    later_sum = (rows > cols).astype(BF16)
    causal = cols < rows
    sign_bit = jnp.uint32(0x80000000)
    lanes = [slice(hh * ATT_HEAD_DIM, (hh + 1) * ATT_HEAD_DIM) for hh in range(heads)]

    def step(i, blocks, start, slot):
        qs = pl.multiple_of(i * t, t)
        starts = [pl.multiple_of(jblk * t, t) for jblk, _ in blocks]
        log_betas, log_remains, ws = [None] * heads, [None] * heads, [None] * heads
        alive = None
        for s in range(heads + 2):
            if s < heads:
                log_betas[s], log_remains[s] = [], []
                for ks, (_, diagonal) in zip(starts, blocks):
                    z = lax.dot_general(q_ref[pl.ds(qs, t), lanes[s]], k_ref[pl.ds(ks, t), lanes[s]],
                                        (((1,), (1,)), ((), ())), preferred_element_type=F32)
                    neg_abs = lax.bitcast_convert_type(lax.bitcast_convert_type(z, jnp.uint32) | sign_bit, F32)
                    log_beta = jnp.minimum(z, 0.0) - jnp.log2(1.0 + jnp.exp2(neg_abs))
                    log_remain = log_beta - z
                    if diagonal:
                        log_remain = jnp.where(causal, log_remain, 0.0)
                    log_betas[s].append(log_beta)
                    log_remains[s].append(log_remain)
            if 0 <= s - 1 < heads:
                hh = s - 1
                ws[hh] = []
                for p, (_, diagonal) in enumerate(blocks):
                    within = jnp.dot(log_remains[hh][p].astype(BF16), later_sum, preferred_element_type=F32)
                    w = jnp.exp2(log_betas[hh][p] + within)
                    if diagonal:
                        w = jnp.where(causal, w, 0.0)
                    ws[hh].append(w.astype(BF16))
            if 0 <= s - 2 < heads:
                hh = s - 2
                r, acc = (None, None) if start else (r_ref[slot, hh], acc_ref[slot, hh])
                for p, ks in enumerate(starts):
                    pv = jnp.dot(ws[hh][p], v_ref[pl.ds(ks, t), lanes[hh]], preferred_element_type=F32)
                    block_sum = jnp.sum(log_remains[hh][p], axis=1, keepdims=True)
                    if r is None:
                        r, acc = block_sum, pv
                    else:
                        r, acc = r + block_sum, acc + jnp.exp2(r) * pv
                r_ref[slot, hh] = r
                acc_ref[slot, hh] = acc
                alive = r if alive is None else jnp.maximum(alive, r)
        return alive

    def any_left(alive):
        return (jnp.max(alive) > LOG2_WEIGHT_FLOOR).astype(jnp.int32)

    def main_step(i, slot):
        return step(i, ((i, True), (i - 1, False)), True, slot)

    def finish(i, slot, alive):
        def cond(c):
            n, go = c
            return jnp.logical_and(n < i, go > 0)

        def body(c):
            n, _ = c
            return n + 1, any_left(step(i, ((i - 1 - n, False),), False, slot))

        lax.while_loop(cond, body, (jnp.int32(1), any_left(alive)))
        for hh in range(heads):
            o_ref[pl.ds(pl.multiple_of(i * t, t), t), lanes[hh]] = acc_ref[slot, hh].astype(o_ref.dtype)

    n_blocks = q_ref.shape[0] // t
    assert n_blocks % 2 == 0
    alive = step(0, ((0, True),), True, 0)
    alive_odd = main_step(1, 1)
    finish(0, 0, alive)

    def block_pair(p, alive_odd):
        i = 2 * p
        alive_even = main_step(i, 0)
        finish(i - 1, 1, alive_odd)
        alive_odd = main_step(i + 1, 1)
        finish(i, 0, alive_even)
        return alive_odd

    alive_odd = lax.fori_loop(1, n_blocks // 2, block_pair, alive_odd)
    finish(n_blocks - 1, 1, alive_odd)


def _attention(proj3):
    b, s, _ = proj3.shape
    width = ATT_HEADS_PER_STEP * ATT_HEAD_DIM
    grid = (b, ATT_HEADS // ATT_HEADS_PER_STEP)
    qb, kb, vb = Q_OFF // width, K_OFF // width, V_OFF // width
    return pl.pallas_call(
        _attn_kernel,
        grid=grid,
        in_specs=[
            pl.BlockSpec((None, s, width), lambda bi, h: (bi, 0, qb + h)),
            pl.BlockSpec((None, s, width), lambda bi, h: (bi, 0, kb + h)),
            pl.BlockSpec((None, s, width), lambda bi, h: (bi, 0, vb + h)),
        ],
        out_specs=pl.BlockSpec((None, s, width), lambda bi, h: (bi, 0, h)),
        out_shape=jax.ShapeDtypeStruct((b, s, ATT_WIDTH), BF16),
        scratch_shapes=[
            pltpu.VMEM((2, ATT_HEADS_PER_STEP, ATT_TILE, 1), F32),
            pltpu.VMEM((2, ATT_HEADS_PER_STEP, ATT_TILE, ATT_HEAD_DIM), F32),
        ],
        compiler_params=_params("arbitrary", "arbitrary"),
        name="attn",
    )(proj3, proj3, proj3)


def _split3(x):
    h1 = x.astype(BF16)
    r1 = x - h1.astype(F32)
    h2 = r1.astype(BF16)
    h3 = (r1 - h2.astype(F32)).astype(BF16)
    return h1, h2, h3


def _dot3(x, m, lhs=True):
    parts = _split3(x)
    if lhs:
        outs = [jnp.dot(p, m, preferred_element_type=F32) for p in parts]
    else:
        outs = [jnp.dot(m, p, preferred_element_type=F32) for p in parts]
    return outs[0] + outs[1] + outs[2]


SSD_CHUNKS_PER_STEP = 8


def _ssd_chunk(first, z, cur, prev, dt_raw, cw_ref, cb_ref, dtb_ref, alog_ref, dskip_ref, gain_ref,
               expand_ref, state_ref):
    L = SSD_CHUNK
    out_t = lax.broadcasted_iota(jnp.int32, ((CONV_WIDTH - 1) * L, 2 * L), 0)
    src_t = lax.broadcasted_iota(jnp.int32, ((CONV_WIDTH - 1) * L, 2 * L), 1)
    shift = out_t // L + 1
    hit = src_t == L + (out_t - (shift - 1) * L) - shift
    shifts = jnp.logical_and(hit, jnp.logical_or(src_t >= L, jnp.logical_not(first))).astype(BF16)
    pieces = []
    col = 0
    for cur_blk, prev_blk in zip(cur, prev):
        width = cur_blk.shape[1]
        both = jnp.concatenate([prev_blk, cur_blk], axis=0)
        back = jnp.dot(shifts, both, preferred_element_type=F32)
        acc = cb_ref[:, col:col + width] + cur_blk.astype(F32) * cw_ref[CONV_WIDTH - 1:CONV_WIDTH, col:col + width]
        for d in range(CONV_WIDTH - 1):
            tap = CONV_WIDTH - 2 - d
            acc = acc + back[d * L:(d + 1) * L] * cw_ref[tap:tap + 1, col:col + width]
        pieces.append(_silu(acc))
        col += width
    xs = jnp.concatenate(pieces[0:2], axis=1)
    b_in = pieces[2][:, 0:SSM_GROUPS * SSM_STATE]
    c_in = pieces[2][:, SSM_GROUPS * SSM_STATE:2 * SSM_GROUPS * SSM_STATE]

    lane = lax.broadcasted_iota(jnp.int32, (1, LANES), 1)
    a2 = jnp.where(lane < SSM_HEADS, -LOG2_E * jnp.exp(alog_ref[...]), 0.0)
    dt_pre = dt_raw + dtb_ref[...]
    dt = jnp.maximum(dt_pre, 0.0) + jnp.log(1.0 + jnp.exp(-jnp.abs(dt_pre)))
    da = dt * a2
    rows = lax.broadcasted_iota(jnp.int32, (L, L), 0)
    cols = lax.broadcasted_iota(jnp.int32, (L, L), 1)
    causal = rows >= cols
    not_causal = jnp.where(causal, 0.0, -jnp.inf)
    a_cum = _dot3(da, causal.astype(BF16), lhs=False)
    src_side = (a_cum - jnp.log2(dt)).T
    a_last = a_cum[L - 1:L, :]
    expand = expand_ref[...]
    decay_in = jnp.dot(jnp.exp2(a_cum).astype(BF16), expand, preferred_element_type=F32)
    w_end = jnp.dot((jnp.exp2(a_last - a_cum) * dt).astype(BF16), expand, preferred_element_type=F32)
    chunk_decay = _dot3(jnp.broadcast_to(jnp.exp2(a_last), (SUBLANES, LANES)), expand)[0:1, :]

    xw = (xs * w_end).astype(BF16)
    lane_in_pair = lax.broadcasted_iota(jnp.int32, (L, LANES), 1)
    y_parts = []
    for g in range(SSM_GROUPS):
        bg = b_in[:, g * SSM_STATE:(g + 1) * SSM_STATE]
        cg = c_in[:, g * SSM_STATE:(g + 1) * SSM_STATE].astype(BF16)
        cb = lax.dot_general(cg, bg.astype(BF16), (((1,), (1,)), ((), ())),
                             preferred_element_type=F32)
        gcols = slice(g * GROUP_WIDTH, (g + 1) * GROUP_WIDTH)
        st = state_ref[g]
        y_off = jnp.dot(cg, st.astype(BF16), preferred_element_type=F32) * decay_in[:, gcols]
        st_new = jnp.dot(bg.T.astype(BF16), xw[:, gcols], preferred_element_type=F32)
        state_ref[g] = st * chunk_decay[:, gcols] + st_new
        heads_per_group = SSM_HEADS // SSM_GROUPS
        for pair in range(heads_per_group // 2):
            h0 = g * heads_per_group + 2 * pair
            slab = xs[:, h0 * SSM_HEAD_DIM:(h0 + 2) * SSM_HEAD_DIM]
            mats = []
            for hh in (h0, h0 + 1):
                seg = (a_cum[:, hh:hh + 1] - src_side[hh:hh + 1, :]) + not_causal
                mats.append((cb * jnp.exp2(seg)).astype(BF16))
            lhs = jnp.concatenate(mats, axis=1)
            rhs = jnp.concatenate(
                [jnp.where(lane_in_pair < SSM_HEAD_DIM, slab, 0.0),
                 jnp.where(lane_in_pair >= SSM_HEAD_DIM, slab, 0.0)], axis=0).astype(BF16)
            y_diag = jnp.dot(lhs, rhs, preferred_element_type=F32)
            lo = (h0 - g * heads_per_group) * SSM_HEAD_DIM
            y_parts.append(y_diag + y_off[:, lo:lo + LANES])
    y = jnp.concatenate(y_parts, axis=1) + xs * dskip_ref[...]
    y = y * _silu(z.astype(F32))
    outs = []
    for g in range(SSM_GROUPS):
        yg = y[:, g * GROUP_WIDTH:(g + 1) * GROUP_WIDTH]
        ms = jnp.mean(yg * yg, axis=-1, keepdims=True)
        outs.append(yg * lax.rsqrt(ms + EPS))
    return (jnp.concatenate(outs, axis=1) * gain_ref[...]).astype(BF16)


def _ssd_kernel(z_ref, x0_ref, x1_ref, bc_ref, p0_ref, p1_ref, pbc_ref, dt_ref, cw_ref, cb_ref, dtb_ref,
                alog_ref, dskip_ref, gain_ref, expand_ref, *rest):
    n_weights = (len(rest) - 2) // 2
    w_refs, o_ref, w_outs, state_ref = (rest[:n_weights], rest[n_weights], rest[n_weights + 1:-1], rest[-1])
    c = pl.program_id(1)
    L = SSD_CHUNK

    for src, dst in zip(w_refs, w_outs):
        dst[...] = src[...].astype(BF16)

    @pl.when(c == 0)
    def _():
        state_ref[...] = jnp.zeros_like(state_ref)

    cur_refs = (x0_ref, x1_ref, bc_ref)
    prev = tuple(r[...] for r in (p0_ref, p1_ref, pbc_ref))
    first = c == 0
    for k in range(SSD_CHUNKS_PER_STEP):
        rows = slice(k * L, (k + 1) * L)
        cur = tuple(r[rows, :] for r in cur_refs)
        o_ref[rows, :] = _ssd_chunk(first, z_ref[rows, :], cur, prev, dt_ref[rows, :], cw_ref, cb_ref, dtb_ref,
                                    alog_ref, dskip_ref, gain_ref, expand_ref, state_ref)
        prev = cur
        first = jnp.bool_(False)


def _ssd(proj3, dt3, conv_w, conv_b, dt_bias, a_log, d_skip_x, gain, expand, weights):
    b, s, _ = proj3.shape
    L = SSD_CHUNK
    per_step = SSD_CHUNKS_PER_STEP if s % (SSD_CHUNKS_PER_STEP * L) == 0 else 1
    assert per_step == SSD_CHUNKS_PER_STEP
    rows = per_step * L
    half = SSM_WIDTH // 2
    grid = (b, s // rows)
    const = lambda shape: pl.BlockSpec(shape, lambda bi, c: (0, 0))
    xbc = lambda k: pl.BlockSpec((None, rows, half), lambda bi, c: (bi, c, XBC_OFF // half + k))
    xbc_prev = lambda k: pl.BlockSpec(
        (None, L, half), lambda bi, c: (bi, jnp.maximum(c * per_step - 1, 0), XBC_OFF // half + k))
    steps = grid[0] * grid[1]

    def slab(w, out):
        rows_total, cols = w.shape[1], w.shape[2]
        n = steps if rows_total % (steps * BF16_SUBLANES) == 0 else steps // 2
        assert rows_total % (n * BF16_SUBLANES) == 0
        per = steps // n
        return pl.BlockSpec((None, rows_total // n, cols),
                            lambda bi, c: (out, (bi * grid[1] + c) // per, 0))
    return pl.pallas_call(
        _ssd_kernel,
        grid=grid,
        in_specs=[
            pl.BlockSpec((None, rows, SSM_WIDTH), lambda bi, c: (bi, c, Z_OFF // SSM_WIDTH)),
            xbc(0), xbc(1), xbc(2),
            xbc_prev(0), xbc_prev(1), xbc_prev(2),
            pl.BlockSpec((None, rows, LANES), lambda bi, c: (bi, c, 0)),
            const((CONV_WIDTH, CONV_DIM)),
            const((1, CONV_DIM)),
            const((1, LANES)),
            const((1, LANES)),
            const((1, SSM_WIDTH)),
            const((1, SSM_WIDTH)),
            const((LANES, SSM_WIDTH)),
        ] + [slab(w, layer) for w, layer in weights],
        out_specs=[pl.BlockSpec((None, rows, SSM_WIDTH), lambda bi, c: (bi, c, 0))]
        + [slab(w, 0) for w, _ in weights],
        out_shape=[jax.ShapeDtypeStruct((b, s, SSM_WIDTH), BF16)]
        + [jax.ShapeDtypeStruct((1,) + w.shape[1:], BF16) for w, _ in weights],
        scratch_shapes=[pltpu.VMEM((SSM_GROUPS, SSM_STATE, GROUP_WIDTH), F32)],
        compiler_params=_params("arbitrary", "arbitrary"),
        name="ssd",
    )(proj3, proj3, proj3, proj3, proj3, proj3, proj3, dt3, conv_w, conv_b, dt_bias, a_log, d_skip_x,
      gain, expand, *[w for w, _ in weights])


OUTPROJ_ROW_GROUPS = 2


def _outproj_kernel(x_ref, oa_ref, os_ref, g_ref, w_ref, o_ref):
    rows = x_ref.shape[0] // OUTPROJ_ROW_GROUPS
    for r in range(OUTPROJ_ROW_GROUPS):
        sl = slice(r * rows, (r + 1) * rows)
        a = oa_ref[sl, :].astype(F32)
        ms = jnp.mean(a * a, axis=-1, keepdims=True)
        an = (a * lax.rsqrt(ms + EPS) * g_ref[...]).astype(BF16)
        y = jnp.dot(an, w_ref[0:ATT_WIDTH, :], preferred_element_type=F32)
        y = y + jnp.dot(os_ref[sl, :], w_ref[ATT_WIDTH:ATT_WIDTH + SSM_WIDTH, :], preferred_element_type=F32)
        o_ref[sl, :] = x_ref[sl, :] + y


def _out_proj(x2, o_att, o_ssm, gain, w_out, layer, *, tm):
    t = x2.shape[0]
    return pl.pallas_call(
        _outproj_kernel,
        grid=(t // tm,),
        in_specs=[
            pl.BlockSpec((tm, D_MODEL), lambda i: (i, 0)),
            pl.BlockSpec((tm, ATT_WIDTH), lambda i: (i, 0)),
            pl.BlockSpec((tm, SSM_WIDTH), lambda i: (i, 0)),
            pl.BlockSpec((1, ATT_WIDTH), lambda i: (0, 0)),
            pl.BlockSpec((None, ATT_WIDTH + SSM_WIDTH, D_MODEL), lambda i: (layer, 0, 0),
                         pipeline_mode=pl.Buffered(1)),
        ],
        out_specs=pl.BlockSpec((tm, D_MODEL), lambda i: (i, 0)),
        out_shape=jax.ShapeDtypeStruct((t, D_MODEL), F32),
        compiler_params=_params("arbitrary"),
        name="out_proj",
    )(x2, o_att, o_ssm, gain, w_out)


FFN_FIRST_STEP_ROW_GROUPS = 2


def _ffn_kernel(x_ref, g_ref, wg_ref, wu_ref, wd_ref, o_ref, h_ref):
    f = pl.program_id(1)

    def swiglu(h):
        gate = jnp.dot(h, wg_ref[...], preferred_element_type=F32)
        up = jnp.dot(h, wu_ref[...], preferred_element_type=F32)
        act = (_silu(gate) * up).astype(BF16)
        return jnp.dot(act, wd_ref[...], preferred_element_type=F32)

    @pl.when(f == 0)
    def _():
        rows = x_ref.shape[0] // FFN_FIRST_STEP_ROW_GROUPS
        for r in range(FFN_FIRST_STEP_ROW_GROUPS):
            sl = slice(r * rows, (r + 1) * rows)
            x = x_ref[sl, :]
            ms = jnp.mean(x * x, axis=-1, keepdims=True)
            h = (x * lax.rsqrt(ms + EPS) * g_ref[...]).astype(BF16)
            h_ref[sl, :] = h
            o_ref[sl, :] = x + swiglu(h)

    @pl.when(f > 0)
    def _():
        o_ref[...] += swiglu(h_ref[...])


def _ffn(x2, g, w_gate, w_up, w_down, layer, *, tm, tf):
    t = x2.shape[0]
    return pl.pallas_call(
        _ffn_kernel,
        grid=(t // tm, D_FF // tf),
        in_specs=[
            pl.BlockSpec((tm, D_MODEL), lambda i, f: (i, 0)),
            pl.BlockSpec((1, D_MODEL), lambda i, f: (0, 0)),
            pl.BlockSpec((None, D_MODEL, tf), lambda i, f: (layer, 0, f)),
            pl.BlockSpec((None, D_MODEL, tf), lambda i, f: (layer, 0, f)),
            pl.BlockSpec((None, tf, D_MODEL), lambda i, f: (layer, f, 0)),
        ],
        out_specs=pl.BlockSpec((tm, D_MODEL), lambda i, f: (i, 0)),
        out_shape=jax.ShapeDtypeStruct((t, D_MODEL), F32),
        scratch_shapes=[pltpu.VMEM((tm, D_MODEL), BF16)],
        compiler_params=_params("arbitrary", "arbitrary"),
        name="ffn",
    )(x2, g, w_gate, w_up, w_down)


def _tile(pref, n):
    return pref if n % pref == 0 else n


def _layer(x, layer, norm_mix, q_gain, k_gain, conv_w, conv_b, dt_bias, a_log, d_skip,
           attn_out_gain, ssm_out_gain, norm_ffn, w_main, w_dt, w_out, w_gate, w_up, w_down, expand):
    b, s, _ = x.shape
    t = b * s
    x2 = x.reshape(t, D_MODEL)
    row = lambda v: v.reshape(1, -1).astype(F32)
    pad_lanes = lambda v: jnp.pad(v.astype(F32), (0, LANES - v.shape[0])).reshape(1, LANES)

    proj, dt_raw = _in_proj(x2, row(norm_mix), w_main, w_dt, layer,
                            row(q_gain) * (LOG2_E * ATT_HEAD_DIM ** -0.5), row(k_gain),
                            tm=_tile(512, t), tn=MAIN_DIM)
    proj3 = proj.reshape(b, s, MAIN_DIM)

    o_att = _attention(proj3)
    to_cast = [(w_out, layer), (w_gate, layer), (w_up, layer), (w_down, layer)]
    o_ssm, w_out, w_gate, w_up, w_down = _ssd(
        proj3, dt_raw.reshape(b, s, LANES), conv_w.astype(F32), row(conv_b), pad_lanes(dt_bias), pad_lanes(a_log),
        row(jnp.repeat(d_skip, SSM_HEAD_DIM)), row(ssm_out_gain), expand, to_cast)

    x2 = _out_proj(x2, o_att.reshape(t, ATT_WIDTH), o_ssm.reshape(t, SSM_WIDTH),
                   row(attn_out_gain), w_out, 0, tm=_tile(1024, t))
    x2 = _ffn(x2, row(norm_ffn), w_gate, w_up, w_down, 0, tm=_tile(1024, t), tf=512)
    return x2.reshape(b, s, D_MODEL)


def _inproj_weights(w_in):
    w_main = w_in.astype(BF16)
    w_dt = jnp.pad(w_in[:, :, MAIN_DIM:], ((0, 0), (0, 0), (0, LANES - SSM_HEADS))).astype(BF16)
    return w_main, w_dt


def _head_expand():
    head_of_lane = jnp.arange(SSM_WIDTH, dtype=jnp.int32) // SSM_HEAD_DIM
    return (jnp.arange(LANES, dtype=jnp.int32)[:, None] == head_of_lane[None, :]).astype(BF16)


def kernel(x, norm_mix, w_in, q_gain, k_gain, conv_w, conv_b, dt_bias, a_log, d_skip, attn_out_gain, ssm_out_gain, w_out, norm_ffn, w_gate, w_up, w_down):
    w_main, w_dt = _inproj_weights(w_in)
    expand = _head_expand()
    for i in range(norm_mix.shape[0]):
        x = _layer(x, i, norm_mix[i], q_gain[i], k_gain[i], conv_w[i], conv_b[i], dt_bias[i], a_log[i],
                   d_skip[i], attn_out_gain[i], ssm_out_gain[i], norm_ffn[i], w_main, w_dt,
                   w_out, w_gate, w_up, w_down, expand)
    return x
```
